```python
import jax, jax.numpy as jnp
from jax import lax
import numpy as np

D_MODEL = 2048
BATCH = 8
SEQ = 4096
DEPTH = 2

CHUNK = 64
GMLP_BLOCK = 128
MIX_WIDTH = D_MODEL
A_WIDTH = MIX_WIDTH // 2
B_WIDTH = MIX_WIDTH - A_WIDTH
A_HEADS = 8
A_HEAD_DIM = A_WIDTH // A_HEADS
B_GROUPS = 8
CONV_WIDTH = 3
IN_COLS = 2 * A_WIDTH + 3 * B_WIDTH
D_FF = -(-8 * D_MODEL // (3 * 256)) * 256
RMS_EPS = 1e-6
LN_EPS = 1e-5

kernel_name = "hybrid_gmlp_shortconv_swiglu_trunk"


def rmsnorm(x, g):
    xf = x.astype(jnp.float32)
    y = xf * lax.rsqrt(jnp.mean(xf * xf, axis=-1, keepdims=True) + RMS_EPS)
    return (y * g.astype(jnp.float32)).astype(x.dtype)


def layernorm(x, g, b):
    xf = x.astype(jnp.float32)
    mu = jnp.mean(xf, axis=-1, keepdims=True)
    xc = xf - mu
    y = xc * lax.rsqrt(jnp.mean(xc * xc, axis=-1, keepdims=True) + LN_EPS)
    return (y * g.astype(jnp.float32) + b.astype(jnp.float32)).astype(x.dtype)


def chunk_causal_block_mask():
    pos = jnp.arange(GMLP_BLOCK)
    return (pos[None, :] // CHUNK) <= (pos[:, None] // CHUNK)


def spatial_gating(u, v, w_s, b_s, ln_g, ln_b):
    bsz, seq, _ = v.shape
    v = layernorm(v, ln_g, ln_b)
    v = v.reshape(bsz, seq // GMLP_BLOCK, GMLP_BLOCK, A_HEADS, A_HEAD_DIM)
    w = jnp.where(chunk_causal_block_mask()[None], w_s, jnp.zeros((), w_s.dtype))
    mixed = jnp.einsum('hij,bcjhd->bcihd', w, v) + b_s.T[None, None, :, :, None]
    return u * mixed.reshape(bsz, seq, A_WIDTH)


def causal_depthwise_conv(h, w):
    c = h.shape[-1]
    return lax.conv_general_dilated(
        h, w[:, None, :].astype(h.dtype), window_strides=(1,),
        padding=[(CONV_WIDTH - 1, 0)],
        dimension_numbers=('NWC', 'WIO', 'NWC'),
        feature_group_count=c)


def hybrid_layer(x, norm1_g, w_in, ln_g, ln_b, w_s, b_s, conv_w, group_norm_g,
                 w_out, norm2_g, w_gate, w_up, w_down):
    h = rmsnorm(x, norm1_g)
    z = jnp.einsum('bsd,dn->bsn', h, w_in)
    z_a = jax.nn.gelu(z[..., :2 * A_WIDTH])
    u, v = z_a[..., :A_WIDTH], z_a[..., A_WIDTH:]
    off = 2 * A_WIDTH
    gate_b = z[..., off:off + B_WIDTH]
    gate_c = z[..., off + B_WIDTH:off + 2 * B_WIDTH]
    h_b = z[..., off + 2 * B_WIDTH:]
    y_a = spatial_gating(u, v, w_s, b_s, ln_g, ln_b)
    y_b = gate_b * causal_depthwise_conv(gate_c * h_b, conv_w)
    y = jnp.concatenate([rmsnorm(y_a, group_norm_g[:A_WIDTH]),
                         rmsnorm(y_b, group_norm_g[A_WIDTH:])], axis=-1)
    x = x + jnp.einsum('bsm,md->bsd', y, w_out)
    h2 = rmsnorm(x, norm2_g)
    act = jax.nn.silu(jnp.einsum('bsd,df->bsf', h2, w_gate)) * jnp.einsum('bsd,df->bsf', h2, w_up)
    return x + jnp.einsum('bsf,fd->bsd', act, w_down)


def setup_inputs(seed: int = 0) -> dict:
    key = jax.random.key(seed)
    ks = jax.random.split(key, 16)
    f32 = jnp.float32
    nrm = lambda k, shape, scale: jax.random.normal(k, shape, f32) * scale
    return {
        "x": jax.random.normal(ks[0], (BATCH, SEQ, D_MODEL), f32),
        "norm1_g": 1.0 + nrm(ks[1], (DEPTH, D_MODEL), 0.02),
        "w_in": nrm(ks[2], (DEPTH, D_MODEL, IN_COLS), D_MODEL ** -0.5),
        "gmlp_ln_g": 1.0 + nrm(ks[3], (DEPTH, A_WIDTH), 0.02),
        "gmlp_ln_b": nrm(ks[4], (DEPTH, A_WIDTH), 0.02),
        "w_spatial": nrm(ks[5], (DEPTH, A_HEADS, GMLP_BLOCK, GMLP_BLOCK), GMLP_BLOCK ** -0.5),
        "b_spatial": 1.0 + nrm(ks[6], (DEPTH, A_HEADS, GMLP_BLOCK), 0.1),
        "conv_w": nrm(ks[7], (DEPTH, CONV_WIDTH, B_WIDTH), CONV_WIDTH ** -0.5),
        "group_norm_g": 1.0 + nrm(ks[8], (DEPTH, MIX_WIDTH), 0.02),
        "w_out": nrm(ks[9], (DEPTH, MIX_WIDTH, D_MODEL), MIX_WIDTH ** -0.5),
        "norm2_g": 1.0 + nrm(ks[10], (DEPTH, D_MODEL), 0.02),
        "w_gate": nrm(ks[11], (DEPTH, D_MODEL, D_FF), D_MODEL ** -0.5),
        "w_up": nrm(ks[12], (DEPTH, D_MODEL, D_FF), D_MODEL ** -0.5),
        "w_down": nrm(ks[13], (DEPTH, D_FF, D_MODEL), D_FF ** -0.5),
        "final_norm_g": 1.0 + nrm(ks[14], (D_MODEL,), 0.02),
    }


def reference(x, norm1_g, w_in, gmlp_ln_g, gmlp_ln_b, w_spatial, b_spatial, conv_w,
              group_norm_g, w_out, norm2_g, w_gate, w_up, w_down, final_norm_g):
    for layer in range(DEPTH):
        x = hybrid_layer(x, norm1_g[layer], w_in[layer], gmlp_ln_g[layer], gmlp_ln_b[layer],
                         w_spatial[layer], b_spatial[layer], conv_w[layer], group_norm_g[layer],
                         w_out[layer], norm2_g[layer], w_gate[layer], w_up[layer], w_down[layer])
    return rmsnorm(x, final_norm_g)
```

```python
import functools

import jax
import jax.numpy as jnp
from jax import lax
from jax.experimental import pallas as pl
from jax.experimental.pallas import tpu as pltpu

RMS_EPS = 1e-6
LN_EPS = 1e-5
CHUNK = 64
GMLP_BLOCK = 128
A_HEADS = 8
CONV_WIDTH = 3
CONV_HALO_ROWS = 8

MIXER_TOKENS = 256
FFN_TOKENS = 512
FFN_CHUNK = 512
MIXER_VMEM_BYTES = 52 * 1024 * 1024
FFN_VMEM_BYTES = 48 * 1024 * 1024

_SQRT_2_OVER_PI = 0.7978845608028654
_F32 = jnp.float32
_BF16 = jnp.bfloat16


def _rms_scale(v):
    return lax.rsqrt(jnp.mean(v * v, axis=-1, keepdims=True) + RMS_EPS)


def _gelu_tanh(v):
    return v * (0.5 * (1.0 + jnp.tanh(_SQRT_2_OVER_PI * (v + 0.044715 * (v * v * v)))))


def _mixer_kernel(x_ref, g1_ref, win_ref, lng_ref, lnb_ref, ws_ref, bs_ref, cw_ref,
                  gn_ref, wout_ref, o_ref, p_s, v_s, u_s, ya_s, y_s, *, tiles_per_seq):
    tm = x_ref.shape[0]
    aw = v_s.shape[1]
    hd = aw // A_HEADS
    blocks = tm // GMLP_BLOCK
    halo = CONV_HALO_ROWS

    x = x_ref[...]
    h = (x * _rms_scale(x) * g1_ref[...]).astype(_BF16)

    def proj(group):
        return jnp.dot(h, win_ref[:, group * aw:(group + 1) * aw],
                       preferred_element_type=_F32)

    @pl.when(pl.program_id(0) % tiles_per_seq == 0)
    def _():
        p_s[0:halo, :] = jnp.zeros((halo, aw), _F32)

    p = proj(3) * proj(4)
    p_s[halo:halo + tm, :] = p
    cw = cw_ref[...]
    conv = (cw[0:1, :] * p_s[halo - 2:halo - 2 + tm, :]
            + cw[1:2, :] * p_s[halo - 1:halo - 1 + tm, :]
            + cw[2:3, :] * p)
    p_s[0:halo, :] = p_s[tm:tm + halo, :]
    y_b = proj(2) * conv
    y_s[:, aw:2 * aw] = (y_b * _rms_scale(y_b) * gn_ref[:, aw:2 * aw]).astype(_BF16)

    v = _gelu_tanh(proj(1))
    vc = v - jnp.mean(v, axis=-1, keepdims=True)
    v_ln = vc * lax.rsqrt(jnp.mean(vc * vc, axis=-1, keepdims=True) + LN_EPS)
    v_s[...] = (v_ln * lng_ref[...] + lnb_ref[...]).astype(_BF16)
    u_s[...] = _gelu_tanh(proj(0))

    pos_i = lax.broadcasted_iota(jnp.int32, (GMLP_BLOCK, GMLP_BLOCK), 0) // CHUNK
    pos_j = lax.broadcasted_iota(jnp.int32, (GMLP_BLOCK, GMLP_BLOCK), 1) // CHUNK
    causal = pos_j <= pos_i
    for head in range(A_HEADS):
        cols = slice(head * hd, (head + 1) * hd)
        w_head = jnp.where(causal, ws_ref[head], jnp.zeros((), _BF16))
        rhs = jnp.concatenate(
            [v_s[r * GMLP_BLOCK:(r + 1) * GMLP_BLOCK, cols] for r in range(blocks)], axis=1)
        mixed = jnp.dot(w_head, rhs, preferred_element_type=_F32) + bs_ref[head]
        for r in range(blocks):
            rows = slice(r * GMLP_BLOCK, (r + 1) * GMLP_BLOCK)
            ya_s[rows, cols] = u_s[rows, cols] * mixed[:, r * hd:(r + 1) * hd]

    y_a = ya_s[...]
    y_s[:, 0:aw] = (y_a * _rms_scale(y_a) * gn_ref[:, 0:aw]).astype(_BF16)

    o_ref[...] = x + jnp.dot(y_s[...], wout_ref[...], preferred_element_type=_F32)


def _ffn_kernel(x_ref, g2_ref, wg_ref, wu_ref, wd_ref, gf_ref, o_ref, h_s, *, final_norm):
    j = pl.program_id(1)

    @pl.when(j == 0)
    def _():
        x = x_ref[...]
        h_s[...] = (x * _rms_scale(x) * g2_ref[...]).astype(_BF16)
        o_ref[...] = x

    h = h_s[...]
    half_gate = 0.5 * jnp.dot(h, wg_ref[...], preferred_element_type=_F32)
    up = jnp.dot(h, wu_ref[...], preferred_element_type=_F32)
    act = (half_gate * (1.0 + jnp.tanh(half_gate)) * up).astype(_BF16)
    o_ref[...] += jnp.dot(act, wd_ref[...], preferred_element_type=_F32)

    if final_norm:
        @pl.when(j == pl.num_programs(1) - 1)
        def _():
            y = o_ref[...]
            o_ref[...] = y * _rms_scale(y) * gf_ref[...]


def _resident(shape):
    zeros = (0,) * len(shape)
    return pl.BlockSpec(shape, lambda *_: zeros, pipeline_mode=pl.Buffered(1))


def _mixer_call(x2d, g1, w_in, ln_g, ln_b, w_s, b_s, conv_w, gn, w_out, *, seq):
    tokens, d = x2d.shape
    aw = ln_g.shape[-1]
    tm = MIXER_TOKENS
    assert seq % tm == 0 and tm % GMLP_BLOCK == 0 and tokens % tm == 0
    row_tile = pl.BlockSpec((tm, d), lambda i: (i, 0))
    return pl.pallas_call(
        functools.partial(_mixer_kernel, tiles_per_seq=seq // tm),
        out_shape=jax.ShapeDtypeStruct((tokens, d), _F32),
        grid=(tokens // tm,),
        in_specs=[row_tile, _resident(g1.shape), _resident(w_in.shape),
                  _resident(ln_g.shape), _resident(ln_b.shape), _resident(w_s.shape),
                  _resident(b_s.shape), _resident(conv_w.shape), _resident(gn.shape),
                  _resident(w_out.shape)],
        out_specs=row_tile,
        scratch_shapes=[
            pltpu.VMEM((tm + CONV_HALO_ROWS, aw), _F32),
            pltpu.VMEM((tm, aw), _BF16),
            pltpu.VMEM((tm, aw), _F32),
            pltpu.VMEM((tm, aw), _F32),
            pltpu.VMEM((tm, 2 * aw), _BF16),
        ],
        compiler_params=pltpu.CompilerParams(
            dimension_semantics=("arbitrary",),
            vmem_limit_bytes=MIXER_VMEM_BYTES),
        name="mixer",
    )(x2d, g1, w_in, ln_g, ln_b, w_s, b_s, conv_w, gn, w_out)


def _ffn_call(x2d, g2, wg, wu, wd, gf, *, final_norm):
    tokens, d = x2d.shape
    nj, _, tf = wg.shape
    tm = FFN_TOKENS
    assert tokens % tm == 0
    row_tile = pl.BlockSpec((tm, d), lambda i, j: (i, 0))
    col_chunk = pl.BlockSpec((None, d, tf), lambda i, j: (j, 0, 0))
    return pl.pallas_call(
        functools.partial(_ffn_kernel, final_norm=final_norm),
        out_shape=jax.ShapeDtypeStruct((tokens, d), _F32),
        grid=(tokens // tm, nj),
        in_specs=[row_tile, _resident(g2.shape), col_chunk, col_chunk,
                  pl.BlockSpec((tf, d), lambda i, j: (j, 0)), _resident(gf.shape)],
        out_specs=row_tile,
        scratch_shapes=[pltpu.VMEM((tm, d), _BF16)],
        compiler_params=pltpu.CompilerParams(
            dimension_semantics=("arbitrary", "arbitrary"),
            vmem_limit_bytes=FFN_VMEM_BYTES),
        name="ffn",
    )(x2d, g2, wg, wu, wd, gf)


def _column_chunks(w, tf):
    d, f = w.shape
    return w.astype(_BF16).reshape(d, f // tf, tf).transpose(1, 0, 2)


def kernel(x, norm1_g, w_in, gmlp_ln_g, gmlp_ln_b, w_spatial, b_spatial, conv_w,
           group_norm_g, w_out, norm2_g, w_gate, w_up, w_down, final_norm_g):
    batch, seq, d = x.shape
    depth = w_in.shape[0]
    d_ff = w_gate.shape[-1]
    assert d_ff % FFN_CHUNK == 0
    row = lambda a: a.reshape(1, -1).astype(_F32)

    xs = x.reshape(batch * seq, d)
    for layer in range(depth):
        xs = _mixer_call(
            xs, row(norm1_g[layer]), w_in[layer].astype(_BF16),
            row(gmlp_ln_g[layer]), row(gmlp_ln_b[layer]),
            w_spatial[layer].astype(_BF16), b_spatial[layer][:, :, None],
            conv_w[layer], row(group_norm_g[layer]), w_out[layer].astype(_BF16), seq=seq)
        xs = _ffn_call(
            xs, row(norm2_g[layer]), _column_chunks(w_gate[layer], FFN_CHUNK),
            _column_chunks(w_up[layer], FFN_CHUNK), w_down[layer].astype(_BF16),
            row(final_norm_g), final_norm=(layer == depth - 1))
    return xs.reshape(batch, seq, d)
```

```python
import functools

import jax
import jax.numpy as jnp
from jax import lax
from jax.experimental import pallas as pl
from jax.experimental.pallas import tpu as pltpu

RMS_EPS = 1e-6
LN_EPS = 1e-5
CHUNK = 64
GMLP_BLOCK = 128
A_HEADS = 8
CONV_HALO_ROWS = 8

MIXER_TOKENS = 256
FFN_TOKENS = 1024
FFN_CHUNK = 512
MIXER_VMEM_BYTES = 52 * 1024 * 1024
FFN_VMEM_BYTES = 56 * 1024 * 1024

_SQRT_2_OVER_PI = 0.7978845608028654
_F32 = jnp.float32
_BF16 = jnp.bfloat16


def _rms_scale(v):
    return lax.rsqrt(jnp.mean(v * v, axis=-1, keepdims=True) + RMS_EPS)


def _gelu_tanh(v):
    return v * (0.5 * (1.0 + jnp.tanh(_SQRT_2_OVER_PI * (v + 0.044715 * (v * v * v)))))


def _mixer_kernel(x_ref, g1_ref, win_ref, lng_ref, lnb_ref, ws_ref, bs_ref, cw_ref,
                  gn_ref, wout_ref, o_ref, p_s, v_s, u_s, ya_s, y_s, *, tiles_per_seq):
    tm = x_ref.shape[0]
    aw = v_s.shape[1]
    hd = aw // A_HEADS
    blocks = tm // GMLP_BLOCK
    halo = CONV_HALO_ROWS

    @pl.when(pl.program_id(0) % tiles_per_seq == 0)
    def _():
        p_s[0:halo, :] = jnp.zeros((halo, aw), _F32)

    x = x_ref[...]
    h = (x * _rms_scale(x) * g1_ref[...]).astype(_BF16)

    def proj(group):
        return jnp.dot(h, win_ref[:, group * aw:(group + 1) * aw],
                       preferred_element_type=_F32)

    v = _gelu_tanh(proj(1))
    vc = v - jnp.mean(v, axis=-1, keepdims=True)
    v_ln = vc * lax.rsqrt(jnp.mean(vc * vc, axis=-1, keepdims=True) + LN_EPS)
    v_s[...] = (v_ln * lng_ref[...] + lnb_ref[...]).astype(_BF16)
    u_s[...] = _gelu_tanh(proj(0))

    pos_i = lax.broadcasted_iota(jnp.int32, (GMLP_BLOCK, GMLP_BLOCK), 0) // CHUNK
    pos_j = lax.broadcasted_iota(jnp.int32, (GMLP_BLOCK, GMLP_BLOCK), 1) // CHUNK
    causal = pos_j <= pos_i
    for head in range(A_HEADS):
        cols = slice(head * hd, (head + 1) * hd)
        w_head = jnp.where(causal, ws_ref[head], jnp.zeros((), _BF16))
        rhs = jnp.concatenate(
            [v_s[r * GMLP_BLOCK:(r + 1) * GMLP_BLOCK, cols] for r in range(blocks)], axis=1)
        mixed = jnp.dot(w_head, rhs, preferred_element_type=_F32) + bs_ref[head]
        for r in range(blocks):
            rows = slice(r * GMLP_BLOCK, (r + 1) * GMLP_BLOCK)
            ya_s[rows, cols] = u_s[rows, cols] * mixed[:, r * hd:(r + 1) * hd]

    y_a = ya_s[...]
    y_s[:, 0:aw] = (y_a * _rms_scale(y_a) * gn_ref[:, 0:aw]).astype(_BF16)

    p = proj(3) * proj(4)
    p_s[halo:halo + tm, :] = p
    cw = cw_ref[...]
    conv = (cw[0:1, :] * p_s[halo - 2:halo - 2 + tm, :]
            + cw[1:2, :] * p_s[halo - 1:halo - 1 + tm, :]
            + cw[2:3, :] * p)
    p_s[0:halo, :] = p_s[tm:tm + halo, :]
    y_b = proj(2) * conv
    y_s[:, aw:2 * aw] = (y_b * _rms_scale(y_b) * gn_ref[:, aw:2 * aw]).astype(_BF16)

    o_ref[...] = x + jnp.dot(y_s[...], wout_ref[...], preferred_element_type=_F32)


def _ffn_kernel(x_hbm, g2_ref, wg_ref, wu_ref, wd_ref, gf_ref, o_ref, x_s, h_s, x_sem,
                *, final_norm):
    i, j = pl.program_id(0), pl.program_id(1)
    tm = o_ref.shape[0]

    def x_copy(tile):
        return pltpu.make_async_copy(x_hbm.at[pl.ds(tile * tm, tm), :], x_s, x_sem)

    @pl.when(j == 0)
    def _():
        @pl.when(i == 0)
        def _():
            x_copy(0).start()

        x_copy(i).wait()
        x = x_s[...]
        h_s[...] = (x * _rms_scale(x) * g2_ref[...]).astype(_BF16)
        o_ref[...] = x

    @pl.when(jnp.logical_and(j == 1, i + 1 < pl.num_programs(0)))
    def _():
        x_copy(i + 1).start()

    h = h_s[...]
    half_gate = 0.5 * jnp.dot(h, wg_ref[...], preferred_element_type=_F32)
    up = jnp.dot(h, wu_ref[...], preferred_element_type=_F32)
    act = (half_gate * (1.0 + jnp.tanh(half_gate)) * up).astype(_BF16)
    o_ref[...] += jnp.dot(act, wd_ref[...], preferred_element_type=_F32)

    if final_norm:
        @pl.when(j == pl.num_programs(1) - 1)
        def _():
            y = o_ref[...]
            o_ref[...] = y * _rms_scale(y) * gf_ref[...]


def _layer_resident(stacked, layer):
    block = (None,) + stacked.shape[1:]
    index = (layer,) + (0,) * (stacked.ndim - 1)
    return pl.BlockSpec(block, lambda *_: index, pipeline_mode=pl.Buffered(1))


def _mixer_call(x2d, layer, g1, w_in, ln_g, ln_b, w_s, b_s, conv_w, gn, w_out, *, seq):
    tokens, d = x2d.shape
    aw = ln_g.shape[-1]
    tm = MIXER_TOKENS
    assert seq % tm == 0 and tm % GMLP_BLOCK == 0 and tokens % tm == 0
    row_tile = pl.BlockSpec((tm, d), lambda i: (i, 0))
    params = (g1, w_in, ln_g, ln_b, w_s, b_s, conv_w, gn, w_out)
    return pl.pallas_call(
        functools.partial(_mixer_kernel, tiles_per_seq=seq // tm),
        out_shape=jax.ShapeDtypeStruct((tokens, d), _F32),
        grid=(tokens // tm,),
        in_specs=[row_tile] + [_layer_resident(p, layer) for p in params],
        out_specs=row_tile,
        scratch_shapes=[
            pltpu.VMEM((tm + CONV_HALO_ROWS, aw), _F32),
            pltpu.VMEM((tm, aw), _BF16),
            pltpu.VMEM((tm, aw), _F32),
            pltpu.VMEM((tm, aw), _F32),
            pltpu.VMEM((tm, 2 * aw), _BF16),
        ],
        compiler_params=pltpu.CompilerParams(
            dimension_semantics=("arbitrary",),
            vmem_limit_bytes=MIXER_VMEM_BYTES),
        name="mixer",
    )(x2d, *params)


def _ffn_call(x2d, layer, g2, wg, wu, wd, gf, *, final_norm):
    tokens, d = x2d.shape
    d_ff = wg.shape[-1]
    tm, tf = FFN_TOKENS, FFN_CHUNK
    assert tokens % tm == 0 and d_ff % tf == 0 and d_ff // tf >= 2
    col_chunk = pl.BlockSpec((None, d, tf), lambda i, j: (layer, 0, j))
    row_chunk = pl.BlockSpec((None, tf, d), lambda i, j: (layer, j, 0))
    return pl.pallas_call(
        functools.partial(_ffn_kernel, final_norm=final_norm),
        out_shape=jax.ShapeDtypeStruct((tokens, d), _F32),
        grid=(tokens // tm, d_ff // tf),
        in_specs=[pl.BlockSpec(memory_space=pl.ANY), _layer_resident(g2, layer),
                  col_chunk, col_chunk, row_chunk,
                  pl.BlockSpec(gf.shape, lambda i, j: (0, 0))],
        out_specs=pl.BlockSpec((tm, d), lambda i, j: (i, 0)),
        scratch_shapes=[
            pltpu.VMEM((tm, d), _F32),
            pltpu.VMEM((tm, d), _BF16),
            pltpu.SemaphoreType.DMA(()),
        ],
        compiler_params=pltpu.CompilerParams(
            dimension_semantics=("arbitrary", "arbitrary"),
            vmem_limit_bytes=FFN_VMEM_BYTES),
        name="ffn",
    )(x2d, g2, wg, wu, wd, gf)


def kernel(x, norm1_g, w_in, gmlp_ln_g, gmlp_ln_b, w_spatial, b_spatial, conv_w,
           group_norm_g, w_out, norm2_g, w_gate, w_up, w_down, final_norm_g):
    batch, seq, d = x.shape
    depth = w_in.shape[0]
    rows = lambda a: a.reshape(depth, 1, -1)
    bf16 = lambda a: a.astype(_BF16)

    mixer_params = (rows(norm1_g), bf16(w_in), rows(gmlp_ln_g), rows(gmlp_ln_b),
                    bf16(w_spatial), b_spatial[..., None], conv_w, rows(group_norm_g),
                    bf16(w_out))
    ffn_params = (rows(norm2_g), bf16(w_gate), bf16(w_up), bf16(w_down),
                  final_norm_g.reshape(1, -1))

    xs = x.reshape(batch * seq, d)
    for layer in range(depth):
        xs = _mixer_call(xs, layer, *mixer_params, seq=seq)
        xs = _ffn_call(xs, layer, *ffn_params, final_norm=(layer == depth - 1))
    return xs.reshape(batch, seq, d)
```

```python
import functools

import jax
import jax.numpy as jnp
from jax import lax
from jax.experimental import pallas as pl
from jax.experimental.pallas import tpu as pltpu

RMS_EPS = 1e-6
LN_EPS = 1e-5
CHUNK = 64
GMLP_BLOCK = 128
A_HEADS = 8
CONV_HALO_ROWS = 8
BF16_TILE_ROWS = 16

MIXER_TOKENS = 256
WEIGHT_COLS = 512
FFN_TOKENS = 1024
FFN_CHUNK = 512
MIXER_VMEM_BYTES = 52 * 1024 * 1024
FFN_VMEM_BYTES = 56 * 1024 * 1024

_SQRT_2_OVER_PI = 0.7978845608028654
_F32 = jnp.float32
_BF16 = jnp.bfloat16


def _rms_scale(v):
    return lax.rsqrt(jnp.mean(v * v, axis=-1, keepdims=True) + RMS_EPS)


def _gelu_tanh(v):
    return v * (0.5 * (1.0 + jnp.tanh(_SQRT_2_OVER_PI * (v + 0.044715 * (v * v * v)))))


def _dot_slabs(lhs, slabs):
    return jnp.concatenate(
        [jnp.dot(lhs, w[...], preferred_element_type=_F32) for w in slabs], axis=1)


def _cast_rows(srcs, dsts):
    for src, dst in zip(srcs, dsts):
        dst[...] = src[...].astype(dst.dtype)


def _mixer_kernel(*refs, tiles_per_seq, in_slabs, out_slabs, casts):
    x_ref, g1_ref, lng_ref, lnb_ref, ws_ref, bs_ref, cw_ref, gn_ref = refs[:8]
    refs = refs[8:]
    win_refs, refs = refs[:in_slabs], refs[in_slabs:]
    wout_refs, refs = refs[:out_slabs], refs[out_slabs:]
    cast_srcs, refs = refs[:casts], refs[casts:]
    o_ref, refs = refs[0], refs[1:]
    cast_dsts, refs = refs[:casts], refs[casts:]
    p_s, v_s, u_s, ya_s, y_s = refs
    tm = x_ref.shape[0]
    aw = v_s.shape[1]
    hd = aw // A_HEADS
    blocks = tm // GMLP_BLOCK
    halo = CONV_HALO_ROWS
    per_group = aw // WEIGHT_COLS

    @pl.when(pl.program_id(0) % tiles_per_seq == 0)
    def _():
        p_s[0:halo, :] = jnp.zeros((halo, aw), _F32)

    _cast_rows(cast_srcs, cast_dsts)

    x = x_ref[...]
    h = (x * _rms_scale(x) * g1_ref[...]).astype(_BF16)

    def proj(group):
        return _dot_slabs(h, win_refs[group * per_group:(group + 1) * per_group])

    v = _gelu_tanh(proj(1))
    vc = v - jnp.mean(v, axis=-1, keepdims=True)
    v_ln = vc * lax.rsqrt(jnp.mean(vc * vc, axis=-1, keepdims=True) + LN_EPS)
    v_s[...] = (v_ln * lng_ref[...] + lnb_ref[...]).astype(_BF16)
    u_s[...] = _gelu_tanh(proj(0))

    pos_i = lax.broadcasted_iota(jnp.int32, (GMLP_BLOCK, GMLP_BLOCK), 0) // CHUNK
    pos_j = lax.broadcasted_iota(jnp.int32, (GMLP_BLOCK, GMLP_BLOCK), 1) // CHUNK
    causal = pos_j <= pos_i
    for head in range(A_HEADS):
        cols = slice(head * hd, (head + 1) * hd)
        w_head = jnp.where(causal, ws_ref[head], jnp.zeros((), _BF16))
        rhs = jnp.concatenate(
            [v_s[r * GMLP_BLOCK:(r + 1) * GMLP_BLOCK, cols] for r in range(blocks)], axis=1)
        mixed = jnp.dot(w_head, rhs, preferred_element_type=_F32) + bs_ref[head]
        for r in range(blocks):
            rows = slice(r * GMLP_BLOCK, (r + 1) * GMLP_BLOCK)
            ya_s[rows, cols] = u_s[rows, cols] * mixed[:, r * hd:(r + 1) * hd]

    y_a = ya_s[...]
    y_s[:, 0:aw] = (y_a * _rms_scale(y_a) * gn_ref[:, 0:aw]).astype(_BF16)

    p = proj(3) * proj(4)
    p_s[halo:halo + tm, :] = p
    cw = cw_ref[...]
    conv = (cw[0:1, :] * p_s[halo - 2:halo - 2 + tm, :]
            + cw[1:2, :] * p_s[halo - 1:halo - 1 + tm, :]
            + cw[2:3, :] * p)
    p_s[0:halo, :] = p_s[tm:tm + halo, :]
    y_b = proj(2) * conv
    y_s[:, aw:2 * aw] = (y_b * _rms_scale(y_b) * gn_ref[:, aw:2 * aw]).astype(_BF16)

    o_ref[...] = x + _dot_slabs(y_s[...], wout_refs)


def _ffn_kernel(*refs, final_norm, casts):
    x_hbm, g2_ref, wg_ref, wu_ref, wd_ref, gf_ref = refs[:6]
    refs = refs[6:]
    cast_srcs, refs = refs[:casts], refs[casts:]
    o_ref, refs = refs[0], refs[1:]
    cast_dsts, refs = refs[:casts], refs[casts:]
    x_s, h_s, x_sem = refs
    i, j = pl.program_id(0), pl.program_id(1)
    tm = o_ref.shape[0]

    def x_copy(tile):
        return pltpu.make_async_copy(x_hbm.at[pl.ds(tile * tm, tm), :], x_s, x_sem)

    def accumulate(base_ref):
        _cast_rows(cast_srcs, cast_dsts)
        h = h_s[...]
        half_gate = 0.5 * jnp.dot(h, wg_ref[...], preferred_element_type=_F32)
        up = jnp.dot(h, wu_ref[...], preferred_element_type=_F32)
        act = (half_gate * (1.0 + jnp.tanh(half_gate)) * up).astype(_BF16)
        o_ref[...] = base_ref[...] + jnp.dot(act, wd_ref[...], preferred_element_type=_F32)

    @pl.when(j == 0)
    def _():
        @pl.when(i == 0)
        def _():
            x_copy(0).start()

        x_copy(i).wait()
        x = x_s[...]
        h_s[...] = (x * _rms_scale(x) * g2_ref[...]).astype(_BF16)
        accumulate(x_s)

    @pl.when(j > 0)
    def _():
        @pl.when(jnp.logical_and(j == 1, i + 1 < pl.num_programs(0)))
        def _():
            x_copy(i + 1).start()

        accumulate(o_ref)

    if final_norm:
        @pl.when(j == pl.num_programs(1) - 1)
        def _():
            y = o_ref[...]
            o_ref[...] = y * _rms_scale(y) * gf_ref[...]


def _layer_resident(stacked, layer):
    block = (None,) + stacked.shape[1:]
    index = (layer,) + (0,) * (stacked.ndim - 1)
    return pl.BlockSpec(block, lambda *_: index, pipeline_mode=pl.Buffered(1))


def _resident_column_slabs(weight):
    rows, cols = weight.shape
    assert cols % WEIGHT_COLS == 0
    return [pl.BlockSpec((rows, WEIGHT_COLS), functools.partial(lambda c, *_: (0, c), c),
                         pipeline_mode=pl.Buffered(1))
            for c in range(cols // WEIGHT_COLS)]


def _cast_job(stacked, layer, step_of, steps):
    _, rows, cols = stacked.shape
    tiles = rows // BF16_TILE_ROWS
    assert rows % BF16_TILE_ROWS == 0 and tiles <= steps
    tile_of = lambda *ids: jnp.minimum(step_of(*ids), tiles - 1)
    src = pl.BlockSpec((None, BF16_TILE_ROWS, cols), lambda *ids: (layer, tile_of(*ids), 0))
    dst = pl.BlockSpec((BF16_TILE_ROWS, cols), lambda *ids: (tile_of(*ids), 0))
    return src, dst, jax.ShapeDtypeStruct((rows, cols), _BF16)


def _mixer_call(x2d, layer, small, w_in, w_out, to_cast, *, seq):
    tokens, d = x2d.shape
    aw = w_out.shape[0] // 2
    tm = MIXER_TOKENS
    assert seq % tm == 0 and tm % GMLP_BLOCK == 0 and tokens % tm == 0
    assert aw % WEIGHT_COLS == 0
    steps = tokens // tm
    row_tile = pl.BlockSpec((tm, d), lambda i: (i, 0))
    in_slabs = _resident_column_slabs(w_in)
    out_slabs = _resident_column_slabs(w_out)
    jobs = [_cast_job(w, layer, lambda i: i, steps) for w in to_cast]
    return pl.pallas_call(
        functools.partial(_mixer_kernel, tiles_per_seq=seq // tm, in_slabs=len(in_slabs),
                          out_slabs=len(out_slabs), casts=len(jobs)),
        out_shape=[jax.ShapeDtypeStruct((tokens, d), _F32)] + [job[2] for job in jobs],
        grid=(steps,),
        in_specs=([row_tile] + [_layer_resident(p, layer) for p in small]
                  + in_slabs + out_slabs + [job[0] for job in jobs]),
        out_specs=[row_tile] + [job[1] for job in jobs],
        scratch_shapes=[
            pltpu.VMEM((tm + CONV_HALO_ROWS, aw), _F32),
            pltpu.VMEM((tm, aw), _BF16),
            pltpu.VMEM((tm, aw), _F32),
            pltpu.VMEM((tm, aw), _F32),
            pltpu.VMEM((tm, 2 * aw), _BF16),
        ],
        compiler_params=pltpu.CompilerParams(
            dimension_semantics=("arbitrary",),
            vmem_limit_bytes=MIXER_VMEM_BYTES),
        name="mixer",
    )(x2d, *small, *([w_in] * len(in_slabs)), *([w_out] * len(out_slabs)), *to_cast)


def _ffn_call(x2d, layer, g2, wg, wu, wd, gf, to_cast, cast_layer, *, final_norm):
    tokens, d = x2d.shape
    d_ff = wg.shape[-1]
    tm, tf = FFN_TOKENS, FFN_CHUNK
    assert tokens % tm == 0 and d_ff % tf == 0 and d_ff // tf >= 2
    grid = (tokens // tm, d_ff // tf)
    jobs = [_cast_job(w, cast_layer, lambda i, j: i * grid[1] + j, grid[0] * grid[1])
            for w in to_cast]
    return pl.pallas_call(
        functools.partial(_ffn_kernel, final_norm=final_norm, casts=len(jobs)),
        out_shape=[jax.ShapeDtypeStruct((tokens, d), _F32)] + [job[2] for job in jobs],
        grid=grid,
        in_specs=[pl.BlockSpec(memory_space=pl.ANY), _layer_resident(g2, layer),
                  pl.BlockSpec((d, tf), lambda i, j: (0, j)),
                  pl.BlockSpec((d, tf), lambda i, j: (0, j)),
                  pl.BlockSpec((tf, d), lambda i, j: (j, 0)),
                  pl.BlockSpec(gf.shape, lambda i, j: (0, 0))] + [job[0] for job in jobs],
        out_specs=[pl.BlockSpec((tm, d), lambda i, j: (i, 0))] + [job[1] for job in jobs],
        scratch_shapes=[
            pltpu.VMEM((tm, d), _F32),
            pltpu.VMEM((tm, d), _BF16),
            pltpu.SemaphoreType.DMA(()),
        ],
        compiler_params=pltpu.CompilerParams(
            dimension_semantics=("arbitrary", "arbitrary"),
            vmem_limit_bytes=FFN_VMEM_BYTES),
        name="ffn",
    )(x2d, g2, wg, wu, wd, gf, *to_cast)


def kernel(x, norm1_g, w_in, gmlp_ln_g, gmlp_ln_b, w_spatial, b_spatial, conv_w,
           group_norm_g, w_out, norm2_g, w_gate, w_up, w_down, final_norm_g):
    batch, seq, d = x.shape
    depth = w_in.shape[0]
    rows = lambda a: a.reshape(depth, 1, -1)

    small = (rows(norm1_g), rows(gmlp_ln_g), rows(gmlp_ln_b), w_spatial.astype(_BF16),
             b_spatial[..., None], conv_w, rows(group_norm_g))
    g2, gf = rows(norm2_g), final_norm_g.reshape(1, -1)

    w_in_l, w_out_l, w_down_l = (w[0].astype(_BF16) for w in (w_in, w_out, w_down))
    xs = x.reshape(batch * seq, d)
    for layer in range(depth):
        last = layer == depth - 1
        xs, w_gate_l, w_up_l = _mixer_call(xs, layer, small, w_in_l, w_out_l,
                                           (w_gate, w_up), seq=seq)
        ahead = () if last else (w_in, w_out, w_down)
        xs, *cast = _ffn_call(xs, layer, g2, w_gate_l, w_up_l, w_down_l, gf, ahead,
                              layer + 1, final_norm=last)
        if not last:
            w_in_l, w_out_l, w_down_l = cast
    return xs.reshape(batch, seq, d)
```

```python
import functools

import jax
import jax.numpy as jnp
from jax import lax
from jax.experimental import pallas as pl
from jax.experimental.pallas import tpu as pltpu

RMS_EPS = 1e-6
LN_EPS = 1e-5
CHUNK = 64
GMLP_BLOCK = 128
A_HEADS = 8
CONV_HALO_ROWS = 8
BF16_TILE_ROWS = 16

MIXER_TOKENS = 256
MIXER_SUBTILES = 2
WEIGHT_COLS = 512
FFN_TOKENS = 1024
FFN_CHUNK = 512
MIXER_VMEM_BYTES = 56 * 1024 * 1024
FFN_VMEM_BYTES = 56 * 1024 * 1024

_SQRT_2_OVER_PI = 0.7978845608028654
_F32 = jnp.float32
_BF16 = jnp.bfloat16


def _rms_scale(v):
    return lax.rsqrt(jnp.mean(v * v, axis=-1, keepdims=True) + RMS_EPS)


def _gelu_tanh(v):
    return v * (0.5 * (1.0 + jnp.tanh(_SQRT_2_OVER_PI * (v + 0.044715 * (v * v * v)))))


def _dot_slabs(lhs, slabs):
    return jnp.concatenate(
        [jnp.dot(lhs, w[...], preferred_element_type=_F32) for w in slabs], axis=1)


def _cast_rows(srcs, dsts):
    for src, dst in zip(srcs, dsts):
        dst[...] = src[...].astype(dst.dtype)


def _mixer_kernel(*refs, tiles_per_seq, in_slabs, out_slabs, casts):
    x_ref, g1_ref, lng_ref, lnb_ref, ws_ref, bs_ref, cw_ref, gn_ref = refs[:8]
    refs = refs[8:]
    win_refs, refs = refs[:in_slabs], refs[in_slabs:]
    wout_refs, refs = refs[:out_slabs], refs[out_slabs:]
    cast_srcs, refs = refs[:casts], refs[casts:]
    o_ref, refs = refs[0], refs[1:]
    cast_dsts, refs = refs[:casts], refs[casts:]
    p_s, v_s, u_s, ya_s, y_s = refs
    tm, aw = v_s.shape
    hd = aw // A_HEADS
    blocks = tm // GMLP_BLOCK
    halo = CONV_HALO_ROWS
    per_group = aw // WEIGHT_COLS

    @pl.when(pl.program_id(0) % tiles_per_seq == 0)
    def _():
        p_s[0:halo, :] = jnp.zeros((halo, aw), _F32)

    _cast_rows(cast_srcs, cast_dsts)

    pos_i = lax.broadcasted_iota(jnp.int32, (GMLP_BLOCK, GMLP_BLOCK), 0) // CHUNK
    pos_j = lax.broadcasted_iota(jnp.int32, (GMLP_BLOCK, GMLP_BLOCK), 1) // CHUNK
    causal = pos_j <= pos_i
    w_heads = [jnp.where(causal, ws_ref[head], jnp.zeros((), _BF16)) for head in range(A_HEADS)]

    for sub in range(x_ref.shape[0] // tm):
        tile = slice(sub * tm, (sub + 1) * tm)
        x = x_ref[tile, :]
        h = (x * _rms_scale(x) * g1_ref[...]).astype(_BF16)

        def proj(group):
            return _dot_slabs(h, win_refs[group * per_group:(group + 1) * per_group])

        v = _gelu_tanh(proj(1))
        vc = v - jnp.mean(v, axis=-1, keepdims=True)
        v_ln = vc * lax.rsqrt(jnp.mean(vc * vc, axis=-1, keepdims=True) + LN_EPS)
        v_s[...] = (v_ln * lng_ref[...] + lnb_ref[...]).astype(_BF16)
        u_s[...] = _gelu_tanh(proj(0))

        for head in range(A_HEADS):
            cols = slice(head * hd, (head + 1) * hd)
            rhs = jnp.concatenate(
                [v_s[r * GMLP_BLOCK:(r + 1) * GMLP_BLOCK, cols] for r in range(blocks)], axis=1)
            mixed = jnp.dot(w_heads[head], rhs, preferred_element_type=_F32) + bs_ref[head]
            for r in range(blocks):
                rows = slice(r * GMLP_BLOCK, (r + 1) * GMLP_BLOCK)
                ya_s[rows, cols] = u_s[rows, cols] * mixed[:, r * hd:(r + 1) * hd]

        y_a = ya_s[...]
        y_s[:, 0:aw] = (y_a * _rms_scale(y_a) * gn_ref[:, 0:aw]).astype(_BF16)

        p = proj(3) * proj(4)
        p_s[halo:halo + tm, :] = p
        cw = cw_ref[...]
        conv = (cw[0:1, :] * p_s[halo - 2:halo - 2 + tm, :]
                + cw[1:2, :] * p_s[halo - 1:halo - 1 + tm, :]
                + cw[2:3, :] * p)
        p_s[0:halo, :] = p_s[tm:tm + halo, :]
        y_b = proj(2) * conv
        y_s[:, aw:2 * aw] = (y_b * _rms_scale(y_b) * gn_ref[:, aw:2 * aw]).astype(_BF16)

        o_ref[tile, :] = x + _dot_slabs(y_s[...], wout_refs)


def _ffn_kernel(*refs, final_norm, casts):
    x_hbm, g2_ref, wg_ref, wu_ref, wd_ref, gf_ref = refs[:6]
    refs = refs[6:]
    cast_srcs, refs = refs[:casts], refs[casts:]
    o_ref, refs = refs[0], refs[1:]
    cast_dsts, refs = refs[:casts], refs[casts:]
    x_s, h_s, x_sem = refs
    i, j = pl.program_id(0), pl.program_id(1)
    tm = o_ref.shape[0]

    def x_copy(tile):
        return pltpu.make_async_copy(x_hbm.at[pl.ds(tile * tm, tm), :], x_s, x_sem)

    def accumulate(base_ref):
        _cast_rows(cast_srcs, cast_dsts)
        h = h_s[...]
        half_gate = 0.5 * jnp.dot(h, wg_ref[...], preferred_element_type=_F32)
        up = jnp.dot(h, wu_ref[...], preferred_element_type=_F32)
        act = (half_gate * (1.0 + jnp.tanh(half_gate)) * up).astype(_BF16)
        o_ref[...] = base_ref[...] + jnp.dot(act, wd_ref[...], preferred_element_type=_F32)

    @pl.when(j == 0)
    def _():
        @pl.when(i == 0)
        def _():
            x_copy(0).start()

        x_copy(i).wait()
        x = x_s[...]
        h_s[...] = (x * _rms_scale(x) * g2_ref[...]).astype(_BF16)
        accumulate(x_s)

    @pl.when(j > 0)
    def _():
        @pl.when(jnp.logical_and(j == 1, i + 1 < pl.num_programs(0)))
        def _():
            x_copy(i + 1).start()

        accumulate(o_ref)

    if final_norm:
        @pl.when(j == pl.num_programs(1) - 1)
        def _():
            y = o_ref[...]
            o_ref[...] = y * _rms_scale(y) * gf_ref[...]


def _layer_resident(stacked, layer):
    block = (None,) + stacked.shape[1:]
    index = (layer,) + (0,) * (stacked.ndim - 1)
    return pl.BlockSpec(block, lambda *_: index, pipeline_mode=pl.Buffered(1))


def _resident_column_slabs(weight):
    rows, cols = weight.shape
    assert cols % WEIGHT_COLS == 0
    return [pl.BlockSpec((rows, WEIGHT_COLS), functools.partial(lambda c, *_: (0, c), c),
                         pipeline_mode=pl.Buffered(1))
            for c in range(cols // WEIGHT_COLS)]


def _cast_job(stacked, layer, step_of, steps):
    _, rows, cols = stacked.shape
    tile_rows = next(r for r in range(BF16_TILE_ROWS, rows + 1, BF16_TILE_ROWS)
                     if rows % r == 0 and rows // r <= steps)
    tiles = rows // tile_rows
    tile_of = lambda *ids: jnp.minimum(step_of(*ids), tiles - 1)
    src = pl.BlockSpec((None, tile_rows, cols), lambda *ids: (layer, tile_of(*ids), 0))
    dst = pl.BlockSpec((tile_rows, cols), lambda *ids: (tile_of(*ids), 0))
    return src, dst, jax.ShapeDtypeStruct((rows, cols), _BF16)


def _mixer_call(x2d, layer, small, w_in, w_out, to_cast, *, seq):
    tokens, d = x2d.shape
    aw = w_out.shape[0] // 2
    tm = MIXER_TOKENS
    step_rows = tm * MIXER_SUBTILES
    assert seq % step_rows == 0 and tm % GMLP_BLOCK == 0 and tokens % step_rows == 0
    assert aw % WEIGHT_COLS == 0
    steps = tokens // step_rows
    row_tile = pl.BlockSpec((step_rows, d), lambda i: (i, 0))
    in_slabs = _resident_column_slabs(w_in)
    out_slabs = _resident_column_slabs(w_out)
    jobs = [_cast_job(w, layer, lambda i: i, steps) for w in to_cast]
    return pl.pallas_call(
        functools.partial(_mixer_kernel, tiles_per_seq=seq // step_rows,
                          in_slabs=len(in_slabs), out_slabs=len(out_slabs), casts=len(jobs)),
        out_shape=[jax.ShapeDtypeStruct((tokens, d), _F32)] + [job[2] for job in jobs],
        grid=(steps,),
        in_specs=([row_tile] + [_layer_resident(p, layer) for p in small]
                  + in_slabs + out_slabs + [job[0] for job in jobs]),
        out_specs=[row_tile] + [job[1] for job in jobs],
        scratch_shapes=[
            pltpu.VMEM((tm + CONV_HALO_ROWS, aw), _F32),
            pltpu.VMEM((tm, aw), _BF16),
            pltpu.VMEM((tm, aw), _F32),
            pltpu.VMEM((tm, aw), _F32),
            pltpu.VMEM((tm, 2 * aw), _BF16),
        ],
        compiler_params=pltpu.CompilerParams(
            dimension_semantics=("arbitrary",),
            vmem_limit_bytes=MIXER_VMEM_BYTES),
        name="mixer",
    )(x2d, *small, *([w_in] * len(in_slabs)), *([w_out] * len(out_slabs)), *to_cast)


def _ffn_call(x2d, layer, g2, wg, wu, wd, gf, to_cast, cast_layer, *, final_norm):
    tokens, d = x2d.shape
    d_ff = wg.shape[-1]
    tm, tf = FFN_TOKENS, FFN_CHUNK
    assert tokens % tm == 0 and d_ff % tf == 0 and d_ff // tf >= 2
    grid = (tokens // tm, d_ff // tf)
    jobs = [_cast_job(w, cast_layer, lambda i, j: i * grid[1] + j, grid[0] * grid[1])
            for w in to_cast]
    return pl.pallas_call(
        functools.partial(_ffn_kernel, final_norm=final_norm, casts=len(jobs)),
        out_shape=[jax.ShapeDtypeStruct((tokens, d), _F32)] + [job[2] for job in jobs],
        grid=grid,
        in_specs=[pl.BlockSpec(memory_space=pl.ANY), _layer_resident(g2, layer),
                  pl.BlockSpec((d, tf), lambda i, j: (0, j)),
                  pl.BlockSpec((d, tf), lambda i, j: (0, j)),
                  pl.BlockSpec((tf, d), lambda i, j: (j, 0)),
                  pl.BlockSpec(gf.shape, lambda i, j: (0, 0))] + [job[0] for job in jobs],
        out_specs=[pl.BlockSpec((tm, d), lambda i, j: (i, 0))] + [job[1] for job in jobs],
        scratch_shapes=[
            pltpu.VMEM((tm, d), _F32),
            pltpu.VMEM((tm, d), _BF16),
            pltpu.SemaphoreType.DMA(()),
        ],
        compiler_params=pltpu.CompilerParams(
            dimension_semantics=("arbitrary", "arbitrary"),
            vmem_limit_bytes=FFN_VMEM_BYTES),
        name="ffn",
    )(x2d, g2, wg, wu, wd, gf, *to_cast)


def kernel(x, norm1_g, w_in, gmlp_ln_g, gmlp_ln_b, w_spatial, b_spatial, conv_w,
           group_norm_g, w_out, norm2_g, w_gate, w_up, w_down, final_norm_g):
    batch, seq, d = x.shape
    depth = w_in.shape[0]
    rows = lambda a: a.reshape(depth, 1, -1)

    small = (rows(norm1_g), rows(gmlp_ln_g), rows(gmlp_ln_b), w_spatial.astype(_BF16),
             b_spatial[..., None], conv_w, rows(group_norm_g))
    g2, gf = rows(norm2_g), final_norm_g.reshape(1, -1)

    w_in_l, w_out_l, w_down_l = (w[0].astype(_BF16) for w in (w_in, w_out, w_down))
    xs = x.reshape(batch * seq, d)
    for layer in range(depth):
        last = layer == depth - 1
        xs, w_gate_l, w_up_l = _mixer_call(xs, layer, small, w_in_l, w_out_l,
                                           (w_gate, w_up), seq=seq)
        ahead = () if last else (w_in, w_out, w_down)
        xs, *cast = _ffn_call(xs, layer, g2, w_gate_l, w_up_l, w_down_l, gf, ahead,
                              layer + 1, final_norm=last)
        if not last:
            w_in_l, w_out_l, w_down_l = cast
    return xs.reshape(batch, seq, d)
```

```python
import functools

import jax
import jax.numpy as jnp
from jax import lax
from jax.experimental import pallas as pl
from jax.experimental.pallas import tpu as pltpu

RMS_EPS = 1e-6
LN_EPS = 1e-5
CHUNK = 64
GMLP_BLOCK = 128
A_HEADS = 8
CONV_HALO_ROWS = 8
BF16_TILE_ROWS = 16

MIXER_TOKENS = 256
MIXER_SUBTILES = 2
WEIGHT_COLS = 512
FFN_TOKENS = 1024
FFN_CHUNK = 512
MIXER_VMEM_BYTES = 56 * 1024 * 1024
FFN_VMEM_BYTES = 56 * 1024 * 1024

_SQRT_2_OVER_PI = 0.7978845608028654
_F32 = jnp.float32
_BF16 = jnp.bfloat16


def _rms_scale(v):
    return lax.rsqrt(jnp.mean(v * v, axis=-1, keepdims=True) + RMS_EPS)


def _gelu_tanh(v):
    return v * (0.5 * (1.0 + jnp.tanh(_SQRT_2_OVER_PI * (v + 0.044715 * (v * v * v)))))


def _dot_slabs(lhs, slabs):
    return jnp.concatenate(
        [jnp.dot(lhs, w[...], preferred_element_type=_F32) for w in slabs], axis=1)


def _cast_rows(srcs, dsts):
    for src, dst in zip(srcs, dsts):
        dst[...] = src[...].astype(dst.dtype)


def _mixer_kernel(*refs, tiles_per_seq, in_slabs, out_slabs, casts):
    x_ref, g1_ref, lng_ref, lnb_ref, ws_ref, bs_ref, cw_ref, gn_ref = refs[:8]
    refs = refs[8:]
    win_refs, refs = refs[:in_slabs], refs[in_slabs:]
    wout_refs, refs = refs[:out_slabs], refs[out_slabs:]
    cast_srcs, refs = refs[:casts], refs[casts:]
    o_ref, refs = refs[0], refs[1:]
    cast_dsts, refs = refs[:casts], refs[casts:]
    p_s, v_s, ya_s, y_s = refs
    tm, aw = v_s.shape
    hd = aw // A_HEADS
    blocks = tm // GMLP_BLOCK
    halo = CONV_HALO_ROWS
    per_group = aw // WEIGHT_COLS

    @pl.when(pl.program_id(0) % tiles_per_seq == 0)
    def _():
        p_s[0:halo, :] = jnp.zeros((halo, aw), _F32)

    _cast_rows(cast_srcs, cast_dsts)

    pos_i = lax.broadcasted_iota(jnp.int32, (GMLP_BLOCK, GMLP_BLOCK), 0) // CHUNK
    pos_j = lax.broadcasted_iota(jnp.int32, (GMLP_BLOCK, GMLP_BLOCK), 1) // CHUNK
    causal = pos_j <= pos_i
    w_heads = [jnp.where(causal, ws_ref[head], jnp.zeros((), _BF16)) for head in range(A_HEADS)]

    for sub in range(x_ref.shape[0] // tm):
        tile = slice(sub * tm, (sub + 1) * tm)
        x = x_ref[tile, :]
        h = (x * _rms_scale(x) * g1_ref[...]).astype(_BF16)

        def proj(group):
            return _dot_slabs(h, win_refs[group * per_group:(group + 1) * per_group])

        v = _gelu_tanh(proj(1))
        vc = v - jnp.mean(v, axis=-1, keepdims=True)
        v_ln = vc * lax.rsqrt(jnp.mean(vc * vc, axis=-1, keepdims=True) + LN_EPS)
        v_s[...] = (v_ln * lng_ref[...] + lnb_ref[...]).astype(_BF16)

        p = proj(3) * proj(4)
        p_s[halo:halo + tm, :] = p
        cw = cw_ref[...]
        conv = (cw[0:1, :] * p_s[halo - 2:halo - 2 + tm, :]
                + cw[1:2, :] * p_s[halo - 1:halo - 1 + tm, :]
                + cw[2:3, :] * p)
        p_s[0:halo, :] = p_s[tm:tm + halo, :]

        mixed = []
        for head in range(A_HEADS):
            cols = slice(head * hd, (head + 1) * hd)
            rhs = jnp.concatenate(
                [v_s[r * GMLP_BLOCK:(r + 1) * GMLP_BLOCK, cols] for r in range(blocks)], axis=1)
            mixed.append(jnp.dot(w_heads[head], rhs, preferred_element_type=_F32) + bs_ref[head])

        u = _gelu_tanh(proj(0))
        for head in range(A_HEADS):
            cols = slice(head * hd, (head + 1) * hd)
            for r in range(blocks):
                rows = slice(r * GMLP_BLOCK, (r + 1) * GMLP_BLOCK)
                ya_s[rows, cols] = u[rows, cols] * mixed[head][:, r * hd:(r + 1) * hd]
        y_a = ya_s[...]
        y_s[:, 0:aw] = (y_a * _rms_scale(y_a) * gn_ref[:, 0:aw]).astype(_BF16)
        y_b = proj(2) * conv
        y_s[:, aw:2 * aw] = (y_b * _rms_scale(y_b) * gn_ref[:, aw:2 * aw]).astype(_BF16)

        o_ref[tile, :] = x + _dot_slabs(y_s[...], wout_refs)


def _ffn_kernel(*refs, final_norm, casts):
    x_hbm, g2_ref, wg_ref, wu_ref, wd_ref, gf_ref = refs[:6]
    refs = refs[6:]
    cast_srcs, refs = refs[:casts], refs[casts:]
    o_ref, refs = refs[0], refs[1:]
    cast_dsts, refs = refs[:casts], refs[casts:]
    x_s, h_s, x_sem = refs
    i, j = pl.program_id(0), pl.program_id(1)
    tm = o_ref.shape[0]

    def x_copy(tile):
        return pltpu.make_async_copy(x_hbm.at[pl.ds(tile * tm, tm), :], x_s, x_sem)

    def accumulate(base_ref):
        _cast_rows(cast_srcs, cast_dsts)
        h = h_s[...]
        half_gate = 0.5 * jnp.dot(h, wg_ref[...], preferred_element_type=_F32)
        up = jnp.dot(h, wu_ref[...], preferred_element_type=_F32)
        act = (half_gate * (1.0 + jnp.tanh(half_gate)) * up).astype(_BF16)
        o_ref[...] = base_ref[...] + jnp.dot(act, wd_ref[...], preferred_element_type=_F32)

    @pl.when(j == 0)
    def _():
        @pl.when(i == 0)
        def _():
            x_copy(0).start()

        x_copy(i).wait()
        x = x_s[...]
        h_s[...] = (x * _rms_scale(x) * g2_ref[...]).astype(_BF16)
        accumulate(x_s)

    @pl.when(j > 0)
    def _():
        @pl.when(jnp.logical_and(j == 1, i + 1 < pl.num_programs(0)))
        def _():
            x_copy(i + 1).start()

        accumulate(o_ref)

    if final_norm:
        @pl.when(j == pl.num_programs(1) - 1)
        def _():
            y = o_ref[...]
            o_ref[...] = y * _rms_scale(y) * gf_ref[...]


def _layer_resident(stacked, layer):
    block = (None,) + stacked.shape[1:]
    index = (layer,) + (0,) * (stacked.ndim - 1)
    return pl.BlockSpec(block, lambda *_: index, pipeline_mode=pl.Buffered(1))


def _resident_column_slabs(weight):
    rows, cols = weight.shape
    assert cols % WEIGHT_COLS == 0
    return [pl.BlockSpec((rows, WEIGHT_COLS), functools.partial(lambda c, *_: (0, c), c),
                         pipeline_mode=pl.Buffered(1))
            for c in range(cols // WEIGHT_COLS)]


def _cast_job(stacked, layer, step_of, steps):
    _, rows, cols = stacked.shape
    tile_rows = next(r for r in range(BF16_TILE_ROWS, rows + 1, BF16_TILE_ROWS)
                     if rows % r == 0 and rows // r <= steps)
    tiles = rows // tile_rows
    tile_of = lambda *ids: jnp.minimum(step_of(*ids), tiles - 1)
    src = pl.BlockSpec((None, tile_rows, cols), lambda *ids: (layer, tile_of(*ids), 0))
    dst = pl.BlockSpec((tile_rows, cols), lambda *ids: (tile_of(*ids), 0))
    return src, dst, jax.ShapeDtypeStruct((rows, cols), _BF16)


def _mixer_call(x2d, layer, small, w_in, w_out, to_cast, *, seq):
    tokens, d = x2d.shape
    aw = w_out.shape[0] // 2
    tm = MIXER_TOKENS
    step_rows = tm * MIXER_SUBTILES
    assert seq % step_rows == 0 and tm % GMLP_BLOCK == 0 and tokens % step_rows == 0
    assert aw % WEIGHT_COLS == 0
    steps = tokens // step_rows
    row_tile = pl.BlockSpec((step_rows, d), lambda i: (i, 0))
    in_slabs = _resident_column_slabs(w_in)
    out_slabs = _resident_column_slabs(w_out)
    jobs = [_cast_job(w, layer, lambda i: i, steps) for w in to_cast]
    return pl.pallas_call(
        functools.partial(_mixer_kernel, tiles_per_seq=seq // step_rows,
                          in_slabs=len(in_slabs), out_slabs=len(out_slabs), casts=len(jobs)),
        out_shape=[jax.ShapeDtypeStruct((tokens, d), _F32)] + [job[2] for job in jobs],
        grid=(steps,),
        in_specs=([row_tile] + [_layer_resident(p, layer) for p in small]
                  + in_slabs + out_slabs + [job[0] for job in jobs]),
        out_specs=[row_tile] + [job[1] for job in jobs],
        scratch_shapes=[
            pltpu.VMEM((tm + CONV_HALO_ROWS, aw), _F32),
            pltpu.VMEM((tm, aw), _BF16),
            pltpu.VMEM((tm, aw), _F32),
            pltpu.VMEM((tm, 2 * aw), _BF16),
        ],
        compiler_params=pltpu.CompilerParams(
            dimension_semantics=("arbitrary",),
            vmem_limit_bytes=MIXER_VMEM_BYTES),
        name="mixer",
    )(x2d, *small, *([w_in] * len(in_slabs)), *([w_out] * len(out_slabs)), *to_cast)


def _ffn_call(x2d, layer, g2, wg, wu, wd, gf, to_cast, cast_layer, *, final_norm):
    tokens, d = x2d.shape
    d_ff = wg.shape[-1]
    tm, tf = FFN_TOKENS, FFN_CHUNK
    assert tokens % tm == 0 and d_ff % tf == 0 and d_ff // tf >= 2
    grid = (tokens // tm, d_ff // tf)
    jobs = [_cast_job(w, cast_layer, lambda i, j: i * grid[1] + j, grid[0] * grid[1])
            for w in to_cast]
    return pl.pallas_call(
        functools.partial(_ffn_kernel, final_norm=final_norm, casts=len(jobs)),
        out_shape=[jax.ShapeDtypeStruct((tokens, d), _F32)] + [job[2] for job in jobs],
        grid=grid,
        in_specs=[pl.BlockSpec(memory_space=pl.ANY), _layer_resident(g2, layer),
                  pl.BlockSpec((d, tf), lambda i, j: (0, j)),
                  pl.BlockSpec((d, tf), lambda i, j: (0, j)),
                  pl.BlockSpec((tf, d), lambda i, j: (j, 0)),
                  pl.BlockSpec(gf.shape, lambda i, j: (0, 0))] + [job[0] for job in jobs],
        out_specs=[pl.BlockSpec((tm, d), lambda i, j: (i, 0))] + [job[1] for job in jobs],
        scratch_shapes=[
            pltpu.VMEM((tm, d), _F32),
            pltpu.VMEM((tm, d), _BF16),
            pltpu.SemaphoreType.DMA(()),
        ],
        compiler_params=pltpu.CompilerParams(
            dimension_semantics=("arbitrary", "arbitrary"),
            vmem_limit_bytes=FFN_VMEM_BYTES),
        name="ffn",
    )(x2d, g2, wg, wu, wd, gf, *to_cast)


def kernel(x, norm1_g, w_in, gmlp_ln_g, gmlp_ln_b, w_spatial, b_spatial, conv_w,
           group_norm_g, w_out, norm2_g, w_gate, w_up, w_down, final_norm_g):
    batch, seq, d = x.shape
    depth = w_in.shape[0]
    rows = lambda a: a.reshape(depth, 1, -1)

    small = (rows(norm1_g), rows(gmlp_ln_g), rows(gmlp_ln_b), w_spatial.astype(_BF16),
             b_spatial[..., None], conv_w, rows(group_norm_g))
    g2, gf = rows(norm2_g), final_norm_g.reshape(1, -1)

    w_in_l, w_out_l, w_down_l = (w[0].astype(_BF16) for w in (w_in, w_out, w_down))
    xs = x.reshape(batch * seq, d)
    for layer in range(depth):
        last = layer == depth - 1
        xs, w_gate_l, w_up_l = _mixer_call(xs, layer, small, w_in_l, w_out_l,
                                           (w_gate, w_up), seq=seq)
        ahead = () if last else (w_in, w_out, w_down)
        xs, *cast = _ffn_call(xs, layer, g2, w_gate_l, w_up_l, w_down_l, gf, ahead,
                              layer + 1, final_norm=last)
        if not last:
            w_in_l, w_out_l, w_down_l = cast
    return xs.reshape(batch, seq, d)
```

```python
import functools

import jax
import jax.numpy as jnp
from jax import lax
from jax.experimental import pallas as pl
from jax.experimental.pallas import tpu as pltpu

RMS_EPS = 1e-6
LN_EPS = 1e-5
CHUNK = 64
GMLP_BLOCK = 128
A_HEADS = 8
CONV_HALO_ROWS = 8
BF16_TILE_ROWS = 16

MIXER_TOKENS = 256
MIXER_SUBTILES = 2
WEIGHT_COLS = 512
FFN_TOKENS = 1024
FFN_CHUNK = 512
MIXER_VMEM_BYTES = 56 * 1024 * 1024
FFN_VMEM_BYTES = 56 * 1024 * 1024

_SQRT_2_OVER_PI = 0.7978845608028654
_F32 = jnp.float32
_BF16 = jnp.bfloat16


def _rms_scale(v):
    return lax.rsqrt(jnp.mean(v * v, axis=-1, keepdims=True) + RMS_EPS)


def _gelu_tanh(v):
    return v * (0.5 * (1.0 + jnp.tanh(_SQRT_2_OVER_PI * (v + 0.044715 * (v * v * v)))))


def _dot_slabs(lhs, slabs):
    return jnp.concatenate(
        [jnp.dot(lhs, w[...], preferred_element_type=_F32) for w in slabs], axis=1)


def _cast_rows(srcs, dsts):
    for src, dst in zip(srcs, dsts):
        dst[...] = src[...].astype(dst.dtype)


def _mixer_kernel(*refs, tiles_per_seq, in_slabs, out_slabs, casts):
    x_ref, g1_ref, lng_ref, lnb_ref, ws_ref, bs_ref, cw_ref, gn_ref = refs[:8]
    refs = refs[8:]
    win_refs, refs = refs[:in_slabs], refs[in_slabs:]
    wout_refs, refs = refs[:out_slabs], refs[out_slabs:]
    cast_srcs, refs = refs[:casts], refs[casts:]
    o_ref, refs = refs[0], refs[1:]
    cast_dsts, refs = refs[:casts], refs[casts:]
    p_s, v_s, ya_s, y_s = refs
    tm, aw = v_s.shape
    hd = aw // A_HEADS
    blocks = tm // GMLP_BLOCK
    halo = CONV_HALO_ROWS
    per_group = aw // WEIGHT_COLS

    @pl.when(pl.program_id(0) % tiles_per_seq == 0)
    def _():
        p_s[0:halo, :] = jnp.zeros((halo, aw), _F32)

    _cast_rows(cast_srcs, cast_dsts)

    pos_i = lax.broadcasted_iota(jnp.int32, (GMLP_BLOCK, GMLP_BLOCK), 0) // CHUNK
    pos_j = lax.broadcasted_iota(jnp.int32, (GMLP_BLOCK, GMLP_BLOCK), 1) // CHUNK
    causal = pos_j <= pos_i
    w_heads = [jnp.where(causal, ws_ref[head], jnp.zeros((), _BF16)) for head in range(A_HEADS)]

    for sub in range(x_ref.shape[0] // tm):
        tile = slice(sub * tm, (sub + 1) * tm)
        x = x_ref[tile, :]
        h = (x * _rms_scale(x) * g1_ref[...]).astype(_BF16)

        def proj(group):
            return _dot_slabs(h, win_refs[group * per_group:(group + 1) * per_group])

        v = _gelu_tanh(proj(1))
        vc = v - jnp.mean(v, axis=-1, keepdims=True)
        v_ln = vc * lax.rsqrt(jnp.mean(vc * vc, axis=-1, keepdims=True) + LN_EPS)
        v_s[...] = (v_ln * lng_ref[...] + lnb_ref[...]).astype(_BF16)

        p = proj(3) * proj(4)
        p_s[halo:halo + tm, :] = p
        cw = cw_ref[...]
        conv = (cw[0:1, :] * p_s[halo - 2:halo - 2 + tm, :]
                + cw[1:2, :] * p_s[halo - 1:halo - 1 + tm, :]
                + cw[2:3, :] * p)
        p_s[0:halo, :] = p_s[tm:tm + halo, :]

        mixed = []
        for head in range(A_HEADS):
            cols = slice(head * hd, (head + 1) * hd)
            rhs = jnp.concatenate(
                [v_s[r * GMLP_BLOCK:(r + 1) * GMLP_BLOCK, cols] for r in range(blocks)], axis=1)
            mixed.append(jnp.dot(w_heads[head], rhs, preferred_element_type=_F32) + bs_ref[head])

        u = _gelu_tanh(proj(0))
        for head in range(A_HEADS):
            cols = slice(head * hd, (head + 1) * hd)
            for r in range(blocks):
                rows = slice(r * GMLP_BLOCK, (r + 1) * GMLP_BLOCK)
                ya_s[rows, cols] = u[rows, cols] * mixed[head][:, r * hd:(r + 1) * hd]
        y_a = ya_s[...]
        y_s[:, 0:aw] = (y_a * _rms_scale(y_a) * gn_ref[:, 0:aw]).astype(_BF16)
        y_b = proj(2) * conv
        y_s[:, aw:2 * aw] = (y_b * _rms_scale(y_b) * gn_ref[:, aw:2 * aw]).astype(_BF16)

        o_ref[tile, :] = x + _dot_slabs(y_s[...], wout_refs)


def _ffn_kernel(*refs, final_norm, casts):
    x_hbm, g2_ref, wg0_ref, wg1_ref, wu0_ref, wu1_ref, wd0_ref, wd1_ref, gf_ref = refs[:9]
    refs = refs[9:]
    cast_srcs, refs = refs[:casts], refs[casts:]
    o_hbm, refs = refs[0], refs[1:]
    cast_dsts, refs = refs[:casts], refs[casts:]
    acc_s, h_s, in_sem, out_sem = refs
    i, j = pl.program_id(0), pl.program_id(1)
    rows, last_j = pl.num_programs(0), pl.num_programs(1) - 1
    tm = h_s.shape[0]
    slot = i % 2
    acc = acc_s.at[slot]

    def fetch(row, to_slot):
        return pltpu.make_async_copy(x_hbm.at[pl.ds(row * tm, tm), :], acc_s.at[to_slot], in_sem)

    def write_back(row, from_slot):
        return pltpu.make_async_copy(acc_s.at[from_slot], o_hbm.at[pl.ds(row * tm, tm), :], out_sem)

    def add_chunk(wg_ref, wu_ref, wd_ref):
        h = h_s[...]
        half_gate = 0.5 * jnp.dot(h, wg_ref[...], preferred_element_type=_F32)
        up = jnp.dot(h, wu_ref[...], preferred_element_type=_F32)
        act = (half_gate * (1.0 + jnp.tanh(half_gate)) * up).astype(_BF16)
        acc[...] += jnp.dot(act, wd_ref[...], preferred_element_type=_F32)

    def add_chunk_pair():
        _cast_rows(cast_srcs, cast_dsts)
        add_chunk(wg0_ref, wu0_ref, wd0_ref)
        add_chunk(wg1_ref, wu1_ref, wd1_ref)

    @pl.when(j == 0)
    def _():
        @pl.when(i == 0)
        def _():
            fetch(0, 0).start()

        fetch(i, slot).wait()
        x = acc[...]
        h_s[...] = (x * _rms_scale(x) * g2_ref[...]).astype(_BF16)
        add_chunk_pair()

    @pl.when(jnp.logical_and(j > 0, j < last_j))
    def _():
        @pl.when(j == 1)
        def _():
            @pl.when(i > 0)
            def _():
                write_back(i - 1, 1 - slot).wait()

            @pl.when(i + 1 < rows)
            def _():
                fetch(i + 1, 1 - slot).start()

        add_chunk_pair()

    @pl.when(j == last_j)
    def _():
        _cast_rows(cast_srcs, cast_dsts)
        add_chunk(wg0_ref, wu0_ref, wd0_ref)
        if final_norm:
            y = acc[...]
            acc[...] = y * _rms_scale(y) * gf_ref[...]
        write_back(i, slot).start()

        @pl.when(i == rows - 1)
        def _():
            write_back(i, slot).wait()


def _layer_resident(stacked, layer):
    block = (None,) + stacked.shape[1:]
    index = (layer,) + (0,) * (stacked.ndim - 1)
    return pl.BlockSpec(block, lambda *_: index, pipeline_mode=pl.Buffered(1))


def _resident_column_slabs(weight):
    rows, cols = weight.shape
    assert cols % WEIGHT_COLS == 0
    return [pl.BlockSpec((rows, WEIGHT_COLS), functools.partial(lambda c, *_: (0, c), c),
                         pipeline_mode=pl.Buffered(1))
            for c in range(cols // WEIGHT_COLS)]


def _cast_job(stacked, layer, step_of, steps):
    _, rows, cols = stacked.shape
    tile_rows = next(r for r in range(BF16_TILE_ROWS, rows + 1, BF16_TILE_ROWS)
                     if rows % r == 0 and rows // r <= steps)
    tiles = rows // tile_rows
    tile_of = lambda *ids: jnp.minimum(step_of(*ids), tiles - 1)
    src = pl.BlockSpec((None, tile_rows, cols), lambda *ids: (layer, tile_of(*ids), 0))
    dst = pl.BlockSpec((tile_rows, cols), lambda *ids: (tile_of(*ids), 0))
    return src, dst, jax.ShapeDtypeStruct((rows, cols), _BF16)


def _mixer_call(x2d, layer, small, w_in, w_out, to_cast, *, seq):
    tokens, d = x2d.shape
    aw = w_out.shape[0] // 2
    tm = MIXER_TOKENS
    step_rows = tm * MIXER_SUBTILES
    assert seq % step_rows == 0 and tm % GMLP_BLOCK == 0 and tokens % step_rows == 0
    assert aw % WEIGHT_COLS == 0
    steps = tokens // step_rows
    row_tile = pl.BlockSpec((step_rows, d), lambda i: (i, 0))
    in_slabs = _resident_column_slabs(w_in)
    out_slabs = _resident_column_slabs(w_out)
    jobs = [_cast_job(w, layer, lambda i: i, steps) for w in to_cast]
    return pl.pallas_call(
        functools.partial(_mixer_kernel, tiles_per_seq=seq // step_rows,
                          in_slabs=len(in_slabs), out_slabs=len(out_slabs), casts=len(jobs)),
        out_shape=[jax.ShapeDtypeStruct((tokens, d), _F32)] + [job[2] for job in jobs],
        grid=(steps,),
        in_specs=([row_tile] + [_layer_resident(p, layer) for p in small]
                  + in_slabs + out_slabs + [job[0] for job in jobs]),
        out_specs=[row_tile] + [job[1] for job in jobs],
        scratch_shapes=[
            pltpu.VMEM((tm + CONV_HALO_ROWS, aw), _F32),
            pltpu.VMEM((tm, aw), _BF16),
            pltpu.VMEM((tm, aw), _F32),
            pltpu.VMEM((tm, 2 * aw), _BF16),
        ],
        compiler_params=pltpu.CompilerParams(
            dimension_semantics=("arbitrary",),
            vmem_limit_bytes=MIXER_VMEM_BYTES),
        name="mixer",
    )(x2d, *small, *([w_in] * len(in_slabs)), *([w_out] * len(out_slabs)), *to_cast)


def _ffn_call(x2d, layer, g2, wg, wu, wd, gf, to_cast, cast_layer, *, final_norm):
    tokens, d = x2d.shape
    d_ff = wg.shape[-1]
    tm, tf = FFN_TOKENS, FFN_CHUNK
    chunks = d_ff // tf
    assert tokens % tm == 0 and d_ff % tf == 0 and chunks % 2 == 1 and chunks >= 5
    grid = (tokens // tm, (chunks + 1) // 2)
    even = lambda j: 2 * j
    odd = lambda j: jnp.minimum(2 * j + 1, chunks - 1)
    jobs = [_cast_job(w, cast_layer, lambda i, j: i * grid[1] + j, grid[0] * grid[1])
            for w in to_cast]
    col_chunk = lambda chunk_of: pl.BlockSpec((d, tf), lambda i, j: (0, chunk_of(j)))
    row_chunk = lambda chunk_of: pl.BlockSpec((tf, d), lambda i, j: (chunk_of(j), 0))
    return pl.pallas_call(
        functools.partial(_ffn_kernel, final_norm=final_norm, casts=len(jobs)),
        out_shape=[jax.ShapeDtypeStruct((tokens, d), _F32)] + [job[2] for job in jobs],
        grid=grid,
        in_specs=[pl.BlockSpec(memory_space=pl.ANY), _layer_resident(g2, layer),
                  col_chunk(even), col_chunk(odd), col_chunk(even), col_chunk(odd),
                  row_chunk(even), row_chunk(odd),
                  pl.BlockSpec(gf.shape, lambda i, j: (0, 0))] + [job[0] for job in jobs],
        out_specs=[pl.BlockSpec(memory_space=pl.ANY)] + [job[1] for job in jobs],
        scratch_shapes=[
            pltpu.VMEM((2, tm, d), _F32),
            pltpu.VMEM((tm, d), _BF16),
            pltpu.SemaphoreType.DMA(()),
            pltpu.SemaphoreType.DMA(()),
        ],
        compiler_params=pltpu.CompilerParams(
            dimension_semantics=("arbitrary", "arbitrary"),
            vmem_limit_bytes=FFN_VMEM_BYTES),
        name="ffn",
    )(x2d, g2, wg, wg, wu, wu, wd, wd, gf, *to_cast)


def kernel(x, norm1_g, w_in, gmlp_ln_g, gmlp_ln_b, w_spatial, b_spatial, conv_w,
           group_norm_g, w_out, norm2_g, w_gate, w_up, w_down, final_norm_g):
    batch, seq, d = x.shape
    depth = w_in.shape[0]
    rows = lambda a: a.reshape(depth, 1, -1)

    small = (rows(norm1_g), rows(gmlp_ln_g), rows(gmlp_ln_b), w_spatial.astype(_BF16),
             b_spatial[..., None], conv_w, rows(group_norm_g))
    g2, gf = rows(norm2_g), final_norm_g.reshape(1, -1)

    w_in_l, w_out_l, w_down_l = (w[0].astype(_BF16) for w in (w_in, w_out, w_down))
    xs = x.reshape(batch * seq, d)
    for layer in range(depth):
        last = layer == depth - 1
        xs, w_gate_l, w_up_l = _mixer_call(xs, layer, small, w_in_l, w_out_l,
                                           (w_gate, w_up), seq=seq)
        ahead = () if last else (w_in, w_out, w_down)
        xs, *cast = _ffn_call(xs, layer, g2, w_gate_l, w_up_l, w_down_l, gf, ahead,
                              layer + 1, final_norm=last)
        if not last:
            w_in_l, w_out_l, w_down_l = cast
    return xs.reshape(batch, seq, d)
```

```python
import functools

import jax
import jax.numpy as jnp
from jax import lax
from jax.experimental import pallas as pl
from jax.experimental.pallas import tpu as pltpu

RMS_EPS = 1e-6
LN_EPS = 1e-5
CHUNK = 64
GMLP_BLOCK = 128
A_HEADS = 8
CONV_HALO_ROWS = 8
BF16_TILE_ROWS = 16

MIXER_TOKENS = 256
MIXER_SUBTILES = 2
WEIGHT_COLS = 512
FFN_TOKENS = 1024
FFN_CHUNK = 512
MIXER_VMEM_BYTES = 58 * 1024 * 1024
FFN_VMEM_BYTES = 56 * 1024 * 1024

_SQRT_2_OVER_PI = 0.7978845608028654
_F32 = jnp.float32
_BF16 = jnp.bfloat16


def _rms_scale(v):
    return lax.rsqrt(jnp.mean(v * v, axis=-1, keepdims=True) + RMS_EPS)


def _gelu_tanh(v):
    return v * (0.5 * (1.0 + jnp.tanh(_SQRT_2_OVER_PI * (v + 0.044715 * (v * v * v)))))


def _dot_slabs(lhs, slabs):
    return jnp.concatenate(
        [jnp.dot(lhs, w[...], preferred_element_type=_F32) for w in slabs], axis=1)


def _cast_rows(srcs, dsts):
    for src, dst in zip(srcs, dsts):
        dst[...] = src[...].astype(dst.dtype)


def _mixer_kernel(*refs, tiles_per_seq, in_slabs, out_slabs, casts):
    x_ref, g1_ref, lng_ref, lnb_ref, ws_ref, bs_ref, cw_ref, gn_ref = refs[:8]
    refs = refs[8:]
    win_refs, refs = refs[:in_slabs], refs[in_slabs:]
    wout_refs, refs = refs[:out_slabs], refs[out_slabs:]
    cast_srcs, refs = refs[:casts], refs[casts:]
    o_ref, refs = refs[0], refs[1:]
    cast_dsts, refs = refs[:casts], refs[casts:]
    p_s, v_s, ya_s, y_s = refs
    tm, aw = v_s.shape
    hd = aw // A_HEADS
    blocks = tm // GMLP_BLOCK
    halo = CONV_HALO_ROWS
    per_group = aw // WEIGHT_COLS

    @pl.when(pl.program_id(0) % tiles_per_seq == 0)
    def _():
        p_s[0:halo, :] = jnp.zeros((halo, aw), _F32)

    _cast_rows(cast_srcs, cast_dsts)

    pos_i = lax.broadcasted_iota(jnp.int32, (GMLP_BLOCK, GMLP_BLOCK), 0) // CHUNK
    pos_j = lax.broadcasted_iota(jnp.int32, (GMLP_BLOCK, GMLP_BLOCK), 1) // CHUNK
    causal = pos_j <= pos_i
    w_heads = [jnp.where(causal, ws_ref[head], jnp.zeros((), _BF16)) for head in range(A_HEADS)]

    for sub in range(x_ref.shape[0] // tm):
        tile = slice(sub * tm, (sub + 1) * tm)
        x = x_ref[tile, :]
        h = (x * _rms_scale(x) * g1_ref[...]).astype(_BF16)

        def proj(group):
            return _dot_slabs(h, win_refs[group * per_group:(group + 1) * per_group])

        v = _gelu_tanh(proj(1))
        vc = v - jnp.mean(v, axis=-1, keepdims=True)
        v_ln = vc * lax.rsqrt(jnp.mean(vc * vc, axis=-1, keepdims=True) + LN_EPS)
        v_s[...] = (v_ln * lng_ref[...] + lnb_ref[...]).astype(_BF16)

        p = proj(3) * proj(4)
        p_s[halo:halo + tm, :] = p
        cw = cw_ref[...]
        conv = (cw[0:1, :] * p_s[halo - 2:halo - 2 + tm, :]
                + cw[1:2, :] * p_s[halo - 1:halo - 1 + tm, :]
                + cw[2:3, :] * p)
        p_s[0:halo, :] = p_s[tm:tm + halo, :]

        mixed = []
        for head in range(A_HEADS):
            cols = slice(head * hd, (head + 1) * hd)
            rhs = jnp.concatenate(
                [v_s[r * GMLP_BLOCK:(r + 1) * GMLP_BLOCK, cols] for r in range(blocks)], axis=1)
            mixed.append(jnp.dot(w_heads[head], rhs, preferred_element_type=_F32) + bs_ref[head])

        u = _gelu_tanh(proj(0))
        for head in range(A_HEADS):
            cols = slice(head * hd, (head + 1) * hd)
            for r in range(blocks):
                rows = slice(r * GMLP_BLOCK, (r + 1) * GMLP_BLOCK)
                ya_s[rows, cols] = u[rows, cols] * mixed[head][:, r * hd:(r + 1) * hd]
        y_a = ya_s[...]
        y_s[:, 0:aw] = (y_a * _rms_scale(y_a) * gn_ref[:, 0:aw]).astype(_BF16)
        y_b = proj(2) * conv
        y_s[:, aw:2 * aw] = (y_b * _rms_scale(y_b) * gn_ref[:, aw:2 * aw]).astype(_BF16)

        o_ref[tile, :] = x + _dot_slabs(y_s[...], wout_refs)


def _ffn_kernel(*refs, final_norm, casts):
    x_hbm, g2_ref, wg0_ref, wg1_ref, wu0_ref, wu1_ref, wd0_ref, wd1_ref, gf_ref = refs[:9]
    refs = refs[9:]
    cast_srcs, refs = refs[:casts], refs[casts:]
    o_hbm, refs = refs[0], refs[1:]
    cast_dsts, refs = refs[:casts], refs[casts:]
    acc_s, h_s, in_sem, out_sem = refs
    i, j = pl.program_id(0), pl.program_id(1)
    rows, last_j = pl.num_programs(0), pl.num_programs(1) - 1
    tm = h_s.shape[0]
    slot = i % 2
    acc = acc_s.at[slot]

    def fetch(row, to_slot):
        return pltpu.make_async_copy(x_hbm.at[pl.ds(row * tm, tm), :], acc_s.at[to_slot], in_sem)

    def write_back(row, from_slot):
        return pltpu.make_async_copy(acc_s.at[from_slot], o_hbm.at[pl.ds(row * tm, tm), :], out_sem)

    def add_chunk(wg_ref, wu_ref, wd_ref):
        h = h_s[...]
        half_gate = 0.5 * jnp.dot(h, wg_ref[...], preferred_element_type=_F32)
        up = jnp.dot(h, wu_ref[...], preferred_element_type=_F32)
        act = (half_gate * (1.0 + jnp.tanh(half_gate)) * up).astype(_BF16)
        acc[...] += jnp.dot(act, wd_ref[...], preferred_element_type=_F32)

    def add_chunk_pair():
        _cast_rows(cast_srcs, cast_dsts)
        add_chunk(wg0_ref, wu0_ref, wd0_ref)
        add_chunk(wg1_ref, wu1_ref, wd1_ref)

    @pl.when(j == 0)
    def _():
        @pl.when(i == 0)
        def _():
            fetch(0, 0).start()

        fetch(i, slot).wait()
        x = acc[...]
        h_s[...] = (x * _rms_scale(x) * g2_ref[...]).astype(_BF16)
        add_chunk_pair()

    @pl.when(jnp.logical_and(j > 0, j < last_j))
    def _():
        @pl.when(j == 1)
        def _():
            @pl.when(i > 0)
            def _():
                write_back(i - 1, 1 - slot).wait()

            @pl.when(i + 1 < rows)
            def _():
                fetch(i + 1, 1 - slot).start()

        add_chunk_pair()

    @pl.when(j == last_j)
    def _():
        _cast_rows(cast_srcs, cast_dsts)
        add_chunk(wg0_ref, wu0_ref, wd0_ref)
        if final_norm:
            y = acc[...]
            acc[...] = y * _rms_scale(y) * gf_ref[...]
        write_back(i, slot).start()

        @pl.when(i == rows - 1)
        def _():
            write_back(i, slot).wait()


def _layer_resident(stacked, layer):
    block = (None,) + stacked.shape[1:]
    index = (layer,) + (0,) * (stacked.ndim - 1)
    return pl.BlockSpec(block, lambda *_: index, pipeline_mode=pl.Buffered(1))


def _resident_column_slabs(weight):
    rows, cols = weight.shape
    assert cols % WEIGHT_COLS == 0
    return [pl.BlockSpec((rows, WEIGHT_COLS), functools.partial(lambda c, *_: (0, c), c),
                         pipeline_mode=pl.Buffered(1))
            for c in range(cols // WEIGHT_COLS)]


def _cast_job(stacked, layer, step_of, steps):
    _, rows, cols = stacked.shape
    tile_rows = next(r for r in range(BF16_TILE_ROWS, rows + 1, BF16_TILE_ROWS)
                     if rows % r == 0 and rows // r <= steps)
    tiles = rows // tile_rows
    tile_of = lambda *ids: jnp.minimum(step_of(*ids), tiles - 1)
    src = pl.BlockSpec((None, tile_rows, cols), lambda *ids: (layer, tile_of(*ids), 0))
    dst = pl.BlockSpec((tile_rows, cols), lambda *ids: (tile_of(*ids), 0))
    return src, dst, jax.ShapeDtypeStruct((rows, cols), _BF16)


def _mixer_call(x2d, layer, small, w_in, w_out, to_cast, *, seq):
    tokens, d = x2d.shape
    aw = w_out.shape[0] // 2
    tm = MIXER_TOKENS
    step_rows = tm * MIXER_SUBTILES
    assert seq % step_rows == 0 and tm % GMLP_BLOCK == 0 and tokens % step_rows == 0
    assert aw % WEIGHT_COLS == 0
    steps = tokens // step_rows
    row_tile = pl.BlockSpec((step_rows, d), lambda i: (i, 0))
    in_slabs = _resident_column_slabs(w_in)
    out_slabs = _resident_column_slabs(w_out)
    jobs = [_cast_job(w, layer, lambda i: i, steps) for w in to_cast]
    return pl.pallas_call(
        functools.partial(_mixer_kernel, tiles_per_seq=seq // step_rows,
                          in_slabs=len(in_slabs), out_slabs=len(out_slabs), casts=len(jobs)),
        out_shape=[jax.ShapeDtypeStruct((tokens, d), _F32)] + [job[2] for job in jobs],
        grid=(steps,),
        in_specs=([row_tile] + [_layer_resident(p, layer) for p in small]
                  + in_slabs + out_slabs + [job[0] for job in jobs]),
        out_specs=[row_tile] + [job[1] for job in jobs],
        scratch_shapes=[
            pltpu.VMEM((tm + CONV_HALO_ROWS, aw), _F32),
            pltpu.VMEM((tm, aw), _BF16),
            pltpu.VMEM((tm, aw), _F32),
            pltpu.VMEM((tm, 2 * aw), _BF16),
        ],
        compiler_params=pltpu.CompilerParams(
            dimension_semantics=("arbitrary",),
            vmem_limit_bytes=MIXER_VMEM_BYTES),
        name="mixer",
    )(x2d, *small, *([w_in] * len(in_slabs)), *([w_out] * len(out_slabs)), *to_cast)


def _ffn_call(x2d, layer, g2, wg, wu, wd, gf, to_cast, cast_layer, *, final_norm):
    tokens, d = x2d.shape
    d_ff = wg.shape[-1]
    tm, tf = FFN_TOKENS, FFN_CHUNK
    chunks = d_ff // tf
    assert tokens % tm == 0 and d_ff % tf == 0 and chunks % 2 == 1 and chunks >= 5
    grid = (tokens // tm, (chunks + 1) // 2)
    even = lambda j: 2 * j
    odd = lambda j: jnp.minimum(2 * j + 1, chunks - 1)
    jobs = [_cast_job(w, cast_layer, lambda i, j: i * grid[1] + j, grid[0] * grid[1])
            for w in to_cast]
    col_chunk = lambda chunk_of: pl.BlockSpec((d, tf), lambda i, j: (0, chunk_of(j)))
    row_chunk = lambda chunk_of: pl.BlockSpec((tf, d), lambda i, j: (chunk_of(j), 0))
    return pl.pallas_call(
        functools.partial(_ffn_kernel, final_norm=final_norm, casts=len(jobs)),
        out_shape=[jax.ShapeDtypeStruct((tokens, d), _F32)] + [job[2] for job in jobs],
        grid=grid,
        in_specs=[pl.BlockSpec(memory_space=pl.ANY), _layer_resident(g2, layer),
                  col_chunk(even), col_chunk(odd), col_chunk(even), col_chunk(odd),
                  row_chunk(even), row_chunk(odd),
                  pl.BlockSpec(gf.shape, lambda i, j: (0, 0))] + [job[0] for job in jobs],
        out_specs=[pl.BlockSpec(memory_space=pl.ANY)] + [job[1] for job in jobs],
        scratch_shapes=[
            pltpu.VMEM((2, tm, d), _F32),
            pltpu.VMEM((tm, d), _BF16),
            pltpu.SemaphoreType.DMA(()),
            pltpu.SemaphoreType.DMA(()),
        ],
        compiler_params=pltpu.CompilerParams(
            dimension_semantics=("arbitrary", "arbitrary"),
            vmem_limit_bytes=FFN_VMEM_BYTES),
        name="ffn",
    )(x2d, g2, wg, wg, wu, wu, wd, wd, gf, *to_cast)


def kernel(x, norm1_g, w_in, gmlp_ln_g, gmlp_ln_b, w_spatial, b_spatial, conv_w,
           group_norm_g, w_out, norm2_g, w_gate, w_up, w_down, final_norm_g):
    batch, seq, d = x.shape
    depth = w_in.shape[0]
    rows = lambda a: a.reshape(depth, 1, -1)

    small = (rows(norm1_g), rows(gmlp_ln_g), rows(gmlp_ln_b), w_spatial.astype(_BF16),
             b_spatial[..., None], conv_w, rows(group_norm_g))
    g2, gf = rows(norm2_g), final_norm_g.reshape(1, -1)

    w_in_l, w_out_l = (w[0].astype(_BF16) for w in (w_in, w_out))
    w_down_l = None
    xs = x.reshape(batch * seq, d)
    for layer in range(depth):
        last = layer == depth - 1
        here = (w_gate, w_up) + ((w_down,) if w_down_l is None else ())
        xs, w_gate_l, w_up_l, *cast = _mixer_call(xs, layer, small, w_in_l, w_out_l, here,
                                                  seq=seq)
        if cast:
            w_down_l, = cast
        ahead = () if last else (w_in, w_out, w_down)
        xs, *cast = _ffn_call(xs, layer, g2, w_gate_l, w_up_l, w_down_l, gf, ahead,
                              layer + 1, final_norm=last)
        if not last:
            w_in_l, w_out_l, w_down_l = cast
    return xs.reshape(batch, seq, d)
```

```python
import functools

import jax
import jax.numpy as jnp
from jax import lax
from jax.experimental import pallas as pl
from jax.experimental.pallas import tpu as pltpu

RMS_EPS = 1e-6
LN_EPS = 1e-5
CHUNK = 64
GMLP_BLOCK = 128
A_HEADS = 8
CONV_HALO_ROWS = 8
BF16_TILE_ROWS = 16

MIXER_TOKENS = 256
MIXER_SUBTILES = 2
WEIGHT_COLS = 512
FFN_TOKENS = 1024
FFN_CHUNK = 512
MIXER_VMEM_BYTES = 58 * 1024 * 1024
FFN_VMEM_BYTES = 56 * 1024 * 1024

_SQRT_2_OVER_PI = 0.7978845608028654
_F32 = jnp.float32
_BF16 = jnp.bfloat16


def _rms_scale(v):
    return lax.rsqrt(jnp.mean(v * v, axis=-1, keepdims=True) + RMS_EPS)


def _gelu_tanh(v):
    return v * (0.5 * (1.0 + jnp.tanh(_SQRT_2_OVER_PI * (v + 0.044715 * (v * v * v)))))


def _dot_slabs(lhs, slabs):
    return jnp.concatenate(
        [jnp.dot(lhs, w[...], preferred_element_type=_F32) for w in slabs], axis=1)


def _cast_rows(srcs, dsts):
    for src, dst in zip(srcs, dsts):
        if len(dst.shape) == 2:
            dst[...] = src[...].astype(dst.dtype)
        else:
            width = dst.shape[2]
            for c in range(dst.shape[0]):
                dst[c] = src[:, c * width:(c + 1) * width].astype(dst.dtype)


def _mixer_kernel(*refs, tiles_per_seq, in_slabs, out_slabs, casts):
    x_ref, g1_ref, lng_ref, lnb_ref, ws_ref, bs_ref, cw_ref, gn_ref = refs[:8]
    refs = refs[8:]
    win_refs, refs = refs[:in_slabs], refs[in_slabs:]
    wout_refs, refs = refs[:out_slabs], refs[out_slabs:]
    cast_srcs, refs = refs[:casts], refs[casts:]
    o_ref, refs = refs[0], refs[1:]
    cast_dsts, refs = refs[:casts], refs[casts:]
    p_s, v_s, ya_s, y_s = refs
    tm, aw = v_s.shape
    hd = aw // A_HEADS
    blocks = tm // GMLP_BLOCK
    halo = CONV_HALO_ROWS
    per_group = aw // WEIGHT_COLS

    @pl.when(pl.program_id(0) % tiles_per_seq == 0)
    def _():
        p_s[0:halo, :] = jnp.zeros((halo, aw), _F32)

    _cast_rows(cast_srcs, cast_dsts)

    pos_i = lax.broadcasted_iota(jnp.int32, (GMLP_BLOCK, GMLP_BLOCK), 0) // CHUNK
    pos_j = lax.broadcasted_iota(jnp.int32, (GMLP_BLOCK, GMLP_BLOCK), 1) // CHUNK
    causal = pos_j <= pos_i
    w_heads = [jnp.where(causal, ws_ref[head], jnp.zeros((), _BF16)) for head in range(A_HEADS)]

    for sub in range(x_ref.shape[0] // tm):
        tile = slice(sub * tm, (sub + 1) * tm)
        x = x_ref[tile, :]
        h = (x * _rms_scale(x) * g1_ref[...]).astype(_BF16)

        def proj(group):
            return _dot_slabs(h, win_refs[group * per_group:(group + 1) * per_group])

        v = _gelu_tanh(proj(1))
        vc = v - jnp.mean(v, axis=-1, keepdims=True)
        v_ln = vc * lax.rsqrt(jnp.mean(vc * vc, axis=-1, keepdims=True) + LN_EPS)
        v_s[...] = (v_ln * lng_ref[...] + lnb_ref[...]).astype(_BF16)

        p = proj(3) * proj(4)
        p_s[halo:halo + tm, :] = p
        cw = cw_ref[...]
        conv = (cw[0:1, :] * p_s[halo - 2:halo - 2 + tm, :]
                + cw[1:2, :] * p_s[halo - 1:halo - 1 + tm, :]
                + cw[2:3, :] * p)
        p_s[0:halo, :] = p_s[tm:tm + halo, :]

        mixed = []
        for head in range(A_HEADS):
            cols = slice(head * hd, (head + 1) * hd)
            rhs = jnp.concatenate(
                [v_s[r * GMLP_BLOCK:(r + 1) * GMLP_BLOCK, cols] for r in range(blocks)], axis=1)
            mixed.append(jnp.dot(w_heads[head], rhs, preferred_element_type=_F32) + bs_ref[head])

        u = _gelu_tanh(proj(0))
        for head in range(A_HEADS):
            cols = slice(head * hd, (head + 1) * hd)
            for r in range(blocks):
                rows = slice(r * GMLP_BLOCK, (r + 1) * GMLP_BLOCK)
                ya_s[rows, cols] = u[rows, cols] * mixed[head][:, r * hd:(r + 1) * hd]
        y_a = ya_s[...]
        y_s[:, 0:aw] = (y_a * _rms_scale(y_a) * gn_ref[:, 0:aw]).astype(_BF16)
        y_b = proj(2) * conv
        y_s[:, aw:2 * aw] = (y_b * _rms_scale(y_b) * gn_ref[:, aw:2 * aw]).astype(_BF16)

        o_ref[tile, :] = x + _dot_slabs(y_s[...], wout_refs)


def _ffn_kernel(*refs, final_norm, casts):
    x_hbm, g2_ref, wg0_ref, wg1_ref, wu0_ref, wu1_ref, wd0_ref, wd1_ref, gf_ref = refs[:9]
    refs = refs[9:]
    cast_srcs, refs = refs[:casts], refs[casts:]
    o_hbm, refs = refs[0], refs[1:]
    cast_dsts, refs = refs[:casts], refs[casts:]
    acc_s, h_s, in_sem, out_sem = refs
    i, j = pl.program_id(0), pl.program_id(1)
    rows, last_j = pl.num_programs(0), pl.num_programs(1) - 1
    tm = h_s.shape[0]
    slot = i % 2
    acc = acc_s.at[slot]

    def fetch(row, to_slot):
        return pltpu.make_async_copy(x_hbm.at[pl.ds(row * tm, tm), :], acc_s.at[to_slot], in_sem)

    def write_back(row, from_slot):
        return pltpu.make_async_copy(acc_s.at[from_slot], o_hbm.at[pl.ds(row * tm, tm), :], out_sem)

    def add_chunk(wg_ref, wu_ref, wd_ref):
        h = h_s[...]
        half_gate = 0.5 * jnp.dot(h, wg_ref[...], preferred_element_type=_F32)
        up = jnp.dot(h, wu_ref[...], preferred_element_type=_F32)
        act = (half_gate * (1.0 + jnp.tanh(half_gate)) * up).astype(_BF16)
        acc[...] += jnp.dot(act, wd_ref[...], preferred_element_type=_F32)

    def add_chunk_pair():
        _cast_rows(cast_srcs, cast_dsts)
        add_chunk(wg0_ref, wu0_ref, wd0_ref)
        add_chunk(wg1_ref, wu1_ref, wd1_ref)

    @pl.when(j == 0)
    def _():
        @pl.when(i == 0)
        def _():
            fetch(0, 0).start()

        fetch(i, slot).wait()
        x = acc[...]
        h_s[...] = (x * _rms_scale(x) * g2_ref[...]).astype(_BF16)
        add_chunk_pair()

    @pl.when(jnp.logical_and(j > 0, j < last_j))
    def _():
        @pl.when(j == 1)
        def _():
            @pl.when(i > 0)
            def _():
                write_back(i - 1, 1 - slot).wait()

            @pl.when(i + 1 < rows)
            def _():
                fetch(i + 1, 1 - slot).start()

        add_chunk_pair()

    @pl.when(j == last_j)
    def _():
        _cast_rows(cast_srcs, cast_dsts)
        add_chunk(wg0_ref, wu0_ref, wd0_ref)
        if final_norm:
            y = acc[...]
            acc[...] = y * _rms_scale(y) * gf_ref[...]
        write_back(i, slot).start()

        @pl.when(i == rows - 1)
        def _():
            write_back(i, slot).wait()


def _layer_resident(stacked, layer):
    block = (None,) + stacked.shape[1:]
    index = (layer,) + (0,) * (stacked.ndim - 1)
    return pl.BlockSpec(block, lambda *_: index, pipeline_mode=pl.Buffered(1))


def _resident_column_slabs(weight):
    rows, cols = weight.shape
    assert cols % WEIGHT_COLS == 0
    return [pl.BlockSpec((rows, WEIGHT_COLS), functools.partial(lambda c, *_: (0, c), c),
                         pipeline_mode=pl.Buffered(1))
            for c in range(cols // WEIGHT_COLS)]


def _cast_job(stacked, col_chunk, layer, step_of, steps):
    _, rows, cols = stacked.shape
    tile_rows = next(r for r in range(BF16_TILE_ROWS, rows + 1, BF16_TILE_ROWS)
                     if rows % r == 0 and rows // r <= steps)
    tiles = rows // tile_rows
    tile_of = lambda *ids: jnp.minimum(step_of(*ids), tiles - 1)
    src = pl.BlockSpec((None, tile_rows, cols), lambda *ids: (layer, tile_of(*ids), 0))
    if col_chunk is None:
        dst = pl.BlockSpec((tile_rows, cols), lambda *ids: (tile_of(*ids), 0))
        return src, dst, jax.ShapeDtypeStruct((rows, cols), _BF16)
    assert cols % col_chunk == 0
    chunks = cols // col_chunk
    dst = pl.BlockSpec((chunks, tile_rows, col_chunk), lambda *ids: (0, tile_of(*ids), 0))
    return src, dst, jax.ShapeDtypeStruct((chunks, rows, col_chunk), _BF16)


def _mixer_call(x2d, layer, small, w_in, w_out, to_cast, *, seq):
    tokens, d = x2d.shape
    aw = w_out.shape[0] // 2
    tm = MIXER_TOKENS
    step_rows = tm * MIXER_SUBTILES
    assert seq % step_rows == 0 and tm % GMLP_BLOCK == 0 and tokens % step_rows == 0
    assert aw % WEIGHT_COLS == 0
    steps = tokens // step_rows
    row_tile = pl.BlockSpec((step_rows, d), lambda i: (i, 0))
    in_slabs = _resident_column_slabs(w_in)
    out_slabs = _resident_column_slabs(w_out)
    jobs = [_cast_job(w, col_chunk, layer, lambda i: i, steps) for w, col_chunk in to_cast]
    return pl.pallas_call(
        functools.partial(_mixer_kernel, tiles_per_seq=seq // step_rows,
                          in_slabs=len(in_slabs), out_slabs=len(out_slabs), casts=len(jobs)),
        out_shape=[jax.ShapeDtypeStruct((tokens, d), _F32)] + [job[2] for job in jobs],
        grid=(steps,),
        in_specs=([row_tile] + [_layer_resident(p, layer) for p in small]
                  + in_slabs + out_slabs + [job[0] for job in jobs]),
        out_specs=[row_tile] + [job[1] for job in jobs],
        scratch_shapes=[
            pltpu.VMEM((tm + CONV_HALO_ROWS, aw), _F32),
            pltpu.VMEM((tm, aw), _BF16),
            pltpu.VMEM((tm, aw), _F32),
            pltpu.VMEM((tm, 2 * aw), _BF16),
        ],
        compiler_params=pltpu.CompilerParams(
            dimension_semantics=("arbitrary",),
            vmem_limit_bytes=MIXER_VMEM_BYTES),
        name="mixer",
    )(x2d, *small, *([w_in] * len(in_slabs)), *([w_out] * len(out_slabs)),
      *[w for w, _ in to_cast])


def _ffn_call(x2d, layer, g2, wg, wu, wd, gf, to_cast, cast_layer, *, final_norm):
    tokens, d = x2d.shape
    tm, tf = FFN_TOKENS, FFN_CHUNK
    chunks = wg.shape[0]
    d_ff = chunks * tf
    assert tokens % tm == 0 and wg.shape[1:] == (d, tf) and wd.shape == (d_ff, d)
    assert chunks % 2 == 1 and chunks >= 5
    grid = (tokens // tm, (chunks + 1) // 2)
    even = lambda j: 2 * j
    odd = lambda j: jnp.minimum(2 * j + 1, chunks - 1)
    jobs = [_cast_job(w, None, cast_layer, lambda i, j: i * grid[1] + j, grid[0] * grid[1])
            for w in to_cast]
    col_chunk = lambda chunk_of: pl.BlockSpec((None, d, tf), lambda i, j: (chunk_of(j), 0, 0))
    row_chunk = lambda chunk_of: pl.BlockSpec((tf, d), lambda i, j: (chunk_of(j), 0))
    return pl.pallas_call(
        functools.partial(_ffn_kernel, final_norm=final_norm, casts=len(jobs)),
        out_shape=[jax.ShapeDtypeStruct((tokens, d), _F32)] + [job[2] for job in jobs],
        grid=grid,
        in_specs=[pl.BlockSpec(memory_space=pl.ANY), _layer_resident(g2, layer),
                  col_chunk(even), col_chunk(odd), col_chunk(even), col_chunk(odd),
                  row_chunk(even), row_chunk(odd),
                  pl.BlockSpec(gf.shape, lambda i, j: (0, 0))] + [job[0] for job in jobs],
        out_specs=[pl.BlockSpec(memory_space=pl.ANY)] + [job[1] for job in jobs],
        scratch_shapes=[
            pltpu.VMEM((2, tm, d), _F32),
            pltpu.VMEM((tm, d), _BF16),
            pltpu.SemaphoreType.DMA(()),
            pltpu.SemaphoreType.DMA(()),
        ],
        compiler_params=pltpu.CompilerParams(
            dimension_semantics=("arbitrary", "arbitrary"),
            vmem_limit_bytes=FFN_VMEM_BYTES),
        name="ffn",
    )(x2d, g2, wg, wg, wu, wu, wd, wd, gf, *to_cast)


def kernel(x, norm1_g, w_in, gmlp_ln_g, gmlp_ln_b, w_spatial, b_spatial, conv_w,
           group_norm_g, w_out, norm2_g, w_gate, w_up, w_down, final_norm_g):
    batch, seq, d = x.shape
    depth = w_in.shape[0]
    rows = lambda a: a.reshape(depth, 1, -1)

    small = (rows(norm1_g), rows(gmlp_ln_g), rows(gmlp_ln_b), w_spatial.astype(_BF16),
             b_spatial[..., None], conv_w, rows(group_norm_g))
    g2, gf = rows(norm2_g), final_norm_g.reshape(1, -1)

    w_in_l, w_out_l = (w[0].astype(_BF16) for w in (w_in, w_out))
    w_down_l = None
    xs = x.reshape(batch * seq, d)
    for layer in range(depth):
        last = layer == depth - 1
        here = ((w_gate, FFN_CHUNK), (w_up, FFN_CHUNK)) + (
            ((w_down, None),) if w_down_l is None else ())
        xs, w_gate_l, w_up_l, *cast = _mixer_call(xs, layer, small, w_in_l, w_out_l, here,
                                                  seq=seq)
        if cast:
            w_down_l, = cast
        ahead = () if last else (w_in, w_out, w_down)
        xs, *cast = _ffn_call(xs, layer, g2, w_gate_l, w_up_l, w_down_l, gf, ahead,
                              layer + 1, final_norm=last)
        if not last:
            w_in_l, w_out_l, w_down_l = cast
    return xs.reshape(batch, seq, d)
```
